```python
import jax, jax.numpy as jnp
from jax import lax
import numpy as np

D_MODEL = 1024
BATCH = 1
SEQ = 16384
DEPTH = 2

HEAD_DIM = 64
A_HEADS = 8
A_KV = 2
B_HEADS = 8
B_KV = 2
C_HEADS = 16
D_FF = 2816
GRID_W = 64
Q_BLOCK = 128
WINDOW = 128
NA_KH = 8
NA_KW = 16
ROPE_THETA = 10000.0
EPS = 1e-6
EVEN_IN = (A_HEADS + 2 * A_KV + B_HEADS + 2 * B_KV) * HEAD_DIM
EVEN_OUT = (A_HEADS + B_HEADS) * HEAD_DIM
ODD_IN = 3 * C_HEADS * HEAD_DIM
ODD_OUT = C_HEADS * HEAD_DIM
NEG_INF = -1e30

kernel_name = "hybrid_axial_window_neighbourhood_encoder"


def rms_norm(x, g):
    xf = x.astype(jnp.float32)
    y = xf * lax.rsqrt(jnp.mean(xf * xf, axis=-1, keepdims=True) + EPS)
    return (y * g.astype(jnp.float32)).astype(x.dtype)


def swiglu(h, w_gate, w_up, w_down):
    return (jax.nn.silu(h @ w_gate) * (h @ w_up)) @ w_down


def rope_cos_sin(pos, dim):
    inv = ROPE_THETA ** (-jnp.arange(0, dim, 2, dtype=jnp.float32) / dim)
    ang = pos.astype(jnp.float32)[:, None] * inv[None, :]
    return jnp.cos(ang), jnp.sin(ang)


def apply_rope(x, cos, sin):
    half = x.shape[-1] // 2
    c = cos[None, :, None, :].astype(x.dtype)
    s = sin[None, :, None, :].astype(x.dtype)
    x1, x2 = x[..., :half], x[..., half:]
    return jnp.concatenate([x1 * c - x2 * s, x1 * s + x2 * c], axis=-1)


def apply_axial_rope(x, cos_r, sin_r, cos_c, sin_c):
    half = x.shape[-1] // 2
    return jnp.concatenate([apply_rope(x[..., :half], cos_r, sin_r),
                            apply_rope(x[..., half:], cos_c, sin_c)], axis=-1)


def global_gqa(q, k, v):
    bsz, s_len, h, d = q.shape
    kv = k.shape[2]
    g = h // kv
    nb = s_len // Q_BLOCK
    scale = HEAD_DIM ** -0.5
    qb = q.reshape(bsz, nb, Q_BLOCK, kv, g, d).transpose(1, 0, 2, 3, 4, 5)

    def block(qi):
        s = jnp.einsum('bqkgd,bskd->bkgqs', qi, k, preferred_element_type=jnp.float32) * scale
        p = jax.nn.softmax(s, axis=-1).astype(v.dtype)
        return jnp.einsum('bkgqs,bskd->bqkgd', p, v)

    o = lax.map(block, qb)
    return o.transpose(1, 0, 2, 3, 4, 5).reshape(bsz, s_len, h * d)


def windowed_gqa_sink(q, k, v, sink):
    bsz, s_len, h, d = q.shape
    kv = k.shape[2]
    g = h // kv
    nb = s_len // Q_BLOCK
    scale = HEAD_DIM ** -0.5
    qb = q.reshape(bsz, nb, Q_BLOCK, kv, g, d)
    pad = ((0, 0), (Q_BLOCK, Q_BLOCK), (0, 0), (0, 0))
    kb = jnp.pad(k, pad).reshape(bsz, nb + 2, Q_BLOCK, kv, d)
    vb = jnp.pad(v, pad).reshape(bsz, nb + 2, Q_BLOCK, kv, d)
    k_slab = jnp.concatenate([kb[:, :-2], kb[:, 1:-1], kb[:, 2:]], axis=2)
    v_slab = jnp.concatenate([vb[:, :-2], vb[:, 1:-1], vb[:, 2:]], axis=2)
    s = jnp.einsum('bnqkgd,bnskd->bnkgqs', qb, k_slab, preferred_element_type=jnp.float32) * scale
    blk = jnp.arange(nb)[:, None, None] * Q_BLOCK
    qpos = blk + jnp.arange(Q_BLOCK)[None, :, None]
    kpos = blk - Q_BLOCK + jnp.arange(3 * Q_BLOCK)[None, None, :]
    valid = (jnp.abs(qpos - kpos) <= WINDOW) & (kpos >= 0) & (kpos < s_len)
    s = jnp.where(valid[None, :, None, None], s, NEG_INF)
    sk = sink.astype(jnp.float32).reshape(kv, g)[None, None, :, :, None]
    m = jnp.maximum(jnp.max(s, axis=-1), sk)
    p = jnp.exp(s - m[..., None])
    denom = jnp.sum(p, axis=-1) + jnp.exp(sk - m)
    p = (p / denom[..., None]).astype(v.dtype)
    o = jnp.einsum('bnkgqs,bnskd->bnqkgd', p, v_slab)
    return o.reshape(bsz, s_len, h * d)


def neighbourhood_attention(q, k, v, rel_bias, rows):
    bsz, s_len, h, d = q.shape
    kh = min(NA_KH, rows)
    kw = NA_KW
    scale = HEAD_DIM ** -0.5
    qg = q.reshape(bsz, rows, GRID_W, h, d)
    kg = k.reshape(bsz, rows, GRID_W, h, d)
    vg = v.reshape(bsz, rows, GRID_W, h, d)
    cols = jnp.arange(GRID_W)
    col_start = jnp.clip(cols - kw // 2, 0, GRID_W - kw)
    col_idx = col_start[:, None] + jnp.arange(kw)[None, :]
    col_bias_idx = col_idx - cols[:, None] + (NA_KW - 1)

    def row_fn(args):
        q_r, r = args
        rs = jnp.clip(r - kh // 2, 0, rows - kh)
        k_rows = lax.dynamic_slice_in_dim(kg, rs, kh, axis=1)
        v_rows = lax.dynamic_slice_in_dim(vg, rs, kh, axis=1)
        k_win = k_rows[:, :, col_idx]
        v_win = v_rows[:, :, col_idx]
        row_bias_idx = rs + jnp.arange(kh) - r + (NA_KH - 1)
        bias = rel_bias[:, row_bias_idx[:, None, None], col_bias_idx[None, :, :]]
        s = jnp.einsum('bchd,bacwhd->bhcaw', q_r, k_win, preferred_element_type=jnp.float32) * scale
        s = s + bias.transpose(0, 2, 1, 3)[None].astype(jnp.float32)
        p = jax.nn.softmax(s.reshape(bsz, h, GRID_W, kh * kw), axis=-1)
        p = p.reshape(s.shape).astype(v.dtype)
        return jnp.einsum('bhcaw,bacwhd->bchd', p, v_win)

    o = lax.map(row_fn, (qg.transpose(1, 0, 2, 3, 4), jnp.arange(rows)))
    return o.transpose(1, 0, 2, 3, 4).reshape(bsz, s_len, h * d)


def even_mixer(h, w_in, q_gain, k_gain, sink, w_out, rope1, rope_axial):
    bsz, s_len, _ = h.shape
    proj = h @ w_in
    sizes = [A_HEADS * HEAD_DIM, A_KV * HEAD_DIM, A_KV * HEAD_DIM,
             B_HEADS * HEAD_DIM, B_KV * HEAD_DIM, B_KV * HEAD_DIM]
    cuts = list(np.cumsum(sizes)[:-1])
    qa, ka, va, qb, kb, vb = jnp.split(proj, cuts, axis=-1)
    hd = lambda t: t.reshape(bsz, s_len, -1, HEAD_DIM)
    qa, ka, va, qb, kb, vb = map(hd, (qa, ka, va, qb, kb, vb))
    qa = apply_axial_rope(rms_norm(qa, q_gain), *rope_axial)
    ka = apply_axial_rope(rms_norm(ka, k_gain), *rope_axial)
    oa = global_gqa(qa, ka, va)
    qb = apply_rope(qb, *rope1)
    kb = apply_rope(kb, *rope1)
    ob = windowed_gqa_sink(qb, kb, vb, sink)
    return jnp.concatenate([oa, ob], axis=-1) @ w_out


def odd_mixer(h, w_qkv, rel_bias, w_out, rows):
    bsz, s_len, _ = h.shape
    q, k, v = jnp.split(h @ w_qkv, 3, axis=-1)
    q = q.reshape(bsz, s_len, C_HEADS, HEAD_DIM)
    k = k.reshape(bsz, s_len, C_HEADS, HEAD_DIM)
    v = v.reshape(bsz, s_len, C_HEADS, HEAD_DIM)
    return neighbourhood_attention(q, k, v, rel_bias, rows) @ w_out


def setup_inputs(seed: int = 0) -> dict:
    key = jax.random.key(seed)
    ks = iter(jax.random.split(key, 32))
    f32 = jnp.float32
    n_even = (DEPTH + 1) // 2
    n_odd = DEPTH // 2

    def w(shape, fan_in):
        return jax.random.normal(next(ks), shape, f32) * (fan_in ** -0.5)

    def gain(shape):
        return 1.0 + 0.1 * jax.random.normal(next(ks), shape, f32)

    return {
        "x": jax.random.normal(next(ks), (BATCH, SEQ, D_MODEL), f32),
        "ffn1_norm": gain((DEPTH, D_MODEL)),
        "ffn1_w_gate": w((DEPTH, D_MODEL, D_FF), D_MODEL),
        "ffn1_w_up": w((DEPTH, D_MODEL, D_FF), D_MODEL),
        "ffn1_w_down": w((DEPTH, D_FF, D_MODEL), D_FF),
        "mix_norm": gain((DEPTH, D_MODEL)),
        "ffn2_norm": gain((DEPTH, D_MODEL)),
        "ffn2_w_gate": w((DEPTH, D_MODEL, D_FF), D_MODEL),
        "ffn2_w_up": w((DEPTH, D_MODEL, D_FF), D_MODEL),
        "ffn2_w_down": w((DEPTH, D_FF, D_MODEL), D_FF),
        "even_w_in": w((n_even, D_MODEL, EVEN_IN), D_MODEL),
        "a_q_norm": gain((n_even, HEAD_DIM)),
        "a_k_norm": gain((n_even, HEAD_DIM)),
        "b_sink": 0.5 * jax.random.normal(next(ks), (n_even, B_HEADS), f32),
        "even_w_out": w((n_even, EVEN_OUT, D_MODEL), EVEN_OUT),
        "odd_w_qkv": w((n_odd, D_MODEL, ODD_IN), D_MODEL),
        "c_rel_bias": 0.1 * jax.random.normal(next(ks), (n_odd, C_HEADS, 2 * NA_KH - 1, 2 * NA_KW - 1), f32),
        "odd_w_out": w((n_odd, ODD_OUT, D_MODEL), ODD_OUT),
        "final_norm": gain((D_MODEL,)),
    }


def reference(x, ffn1_norm, ffn1_w_gate, ffn1_w_up, ffn1_w_down, mix_norm,
              ffn2_norm, ffn2_w_gate, ffn2_w_up, ffn2_w_down,
              even_w_in, a_q_norm, a_k_norm, b_sink, even_w_out,
              odd_w_qkv, c_rel_bias, odd_w_out, final_norm):
    s_len = x.shape[1]
    rows = s_len // GRID_W
    pos = jnp.arange(s_len)
    rope1 = rope_cos_sin(pos, HEAD_DIM)
    cos_r, sin_r = rope_cos_sin(pos // GRID_W, HEAD_DIM // 2)
    cos_c, sin_c = rope_cos_sin(pos % GRID_W, HEAD_DIM // 2)
    rope_axial = (cos_r, sin_r, cos_c, sin_c)
    for layer in range(DEPTH):
        i = layer // 2
        h = rms_norm(x, ffn1_norm[layer])
        x = x + 0.5 * swiglu(h, ffn1_w_gate[layer], ffn1_w_up[layer], ffn1_w_down[layer])
        h = rms_norm(x, mix_norm[layer])
        if layer % 2 == 0:
            x = x + even_mixer(h, even_w_in[i], a_q_norm[i], a_k_norm[i], b_sink[i],
                               even_w_out[i], rope1, rope_axial)
        else:
            x = x + odd_mixer(h, odd_w_qkv[i], c_rel_bias[i], odd_w_out[i], rows)
        h = rms_norm(x, ffn2_norm[layer])
        x = x + 0.5 * swiglu(h, ffn2_w_gate[layer], ffn2_w_up[layer], ffn2_w_down[layer])
    return rms_norm(x, final_norm)
```

```python
import functools

import jax
import jax.numpy as jnp
import numpy as np
from jax import lax
from jax.experimental import pallas as pl
from jax.experimental.pallas import tpu as pltpu

D_MODEL = 1024
SEQ = 16384
DEPTH = 2
HEAD_DIM = 64
A_HEADS = 8
A_KV = 2
B_HEADS = 8
B_KV = 2
C_HEADS = 16
D_FF = 2816
GRID_W = 64
ROWS = SEQ // GRID_W
Q_BLOCK = 128
WINDOW = 128
NA_KH = 8
NA_KW = 16
ROPE_THETA = 10000.0
EPS = 1e-6
EVEN_IN = (A_HEADS + 2 * A_KV + B_HEADS + 2 * B_KV) * HEAD_DIM
NEG_INF = -1e30
SCALE = HEAD_DIM ** -0.5

LANES = 128
GROUP = A_HEADS // A_KV

F32 = jnp.float32
BF16 = jnp.bfloat16

VMEM_LIMIT = 56 * 1024 * 1024


def _cparams(*sem):
    return pltpu.CompilerParams(dimension_semantics=sem, vmem_limit_bytes=VMEM_LIMIT)


def _rms(x, g):
    ms = jnp.mean(x * x, axis=-1, keepdims=True)
    return x * lax.rsqrt(ms + EPS) * g


FFN_TM = 512
FFN_CHUNK = D_FF // 2


def _ffn_kernel(x_ref, g_ref, wg_ref, wu_ref, wd_ref, fg_ref, o_ref, *, final_norm):
    x = x_ref[...]
    h = _rms(x, g_ref[...]).astype(BF16)
    acc = jnp.zeros_like(x)
    for c in range(D_FF // FFN_CHUNK):
        sl = slice(c * FFN_CHUNK, (c + 1) * FFN_CHUNK)
        gate = jnp.dot(h, wg_ref[:, sl], preferred_element_type=F32)
        up = jnp.dot(h, wu_ref[:, sl], preferred_element_type=F32)
        act = (gate * jax.nn.sigmoid(gate) * up).astype(BF16)
        acc = acc + jnp.dot(act, wd_ref[sl, :], preferred_element_type=F32)
    y = x + 0.5 * acc
    if final_norm:
        y = _rms(y, fg_ref[...])
    o_ref[...] = y


def _ffn(x, g, wg, wu, wd, fg, final_norm):
    const = lambda i: (0, 0)
    return pl.pallas_call(
        functools.partial(_ffn_kernel, final_norm=final_norm),
        grid=(SEQ // FFN_TM,),
        in_specs=[
            pl.BlockSpec((FFN_TM, D_MODEL), lambda i: (i, 0)),
            pl.BlockSpec((1, D_MODEL), const),
            pl.BlockSpec((D_MODEL, D_FF), const, pipeline_mode=pl.Buffered(1)),
            pl.BlockSpec((D_MODEL, D_FF), const, pipeline_mode=pl.Buffered(1)),
            pl.BlockSpec((D_FF, D_MODEL), const, pipeline_mode=pl.Buffered(1)),
            pl.BlockSpec((1, D_MODEL), const),
        ],
        out_specs=pl.BlockSpec((FFN_TM, D_MODEL), lambda i: (i, 0)),
        out_shape=jax.ShapeDtypeStruct((SEQ, D_MODEL), F32),
        compiler_params=_cparams("arbitrary"),
        name="ffn",
    )(x, g, wg, wu, wd, fg)


PROJ_TM = 512


def _split_heads(dst_ref, y, first_head):
    for t in range(y.shape[1] // HEAD_DIM):
        dst_ref[first_head + t] = y[:, t * HEAD_DIM:(t + 1) * HEAD_DIM].astype(BF16)


def _even_proj_kernel(x_ref, g_ref, w_ref, gq_ref, gk_ref, gm_ref, ca_ref, sa_ref, cb_ref, sb_ref,
                      qa_ref, ka_ref, va_ref, qb_ref, kb_ref, vb_ref):
    h = _rms(x_ref[...], g_ref[...]).astype(BF16)
    proj = jnp.dot(h, w_ref[...], preferred_element_type=F32)
    lane = lax.broadcasted_iota(jnp.int32, (PROJ_TM, LANES), 1)
    gm = gm_ref[...]

    def head_norm(y, gain):
        sq = y * y
        hi = sq.astype(BF16)
        lo = (sq - hi.astype(F32)).astype(BF16)
        ms = (jnp.dot(hi, gm, preferred_element_type=F32)
              + jnp.dot(lo, gm, preferred_element_type=F32))
        return y * lax.rsqrt(ms + EPS) * gain

    def rope(y, c, s, half):
        fwd = pltpu.roll(y, LANES - half, 1)
        bwd = pltpu.roll(y, half, 1)
        partner = jnp.where((lane % (2 * half)) < half, fwd, bwd)
        return y * c + partner * s

    ca, sa, cb, sb = ca_ref[...], sa_ref[...], cb_ref[...], sb_ref[...]
    col = 0
    for j in range(A_HEADS // 2):
        y = rope(head_norm(proj[:, col:col + LANES], gq_ref[...]), ca, sa, HEAD_DIM // 4) * SCALE
        _split_heads(qa_ref, y, 2 * j)
        col += LANES
    y = rope(head_norm(proj[:, col:col + LANES], gk_ref[...]), ca, sa, HEAD_DIM // 4)
    _split_heads(ka_ref, y, 0)
    col += LANES
    _split_heads(va_ref, proj[:, col:col + LANES], 0)
    col += LANES
    for j in range(B_HEADS // 2):
        y = rope(proj[:, col:col + LANES], cb, sb, HEAD_DIM // 2) * SCALE
        _split_heads(qb_ref, y, 2 * j)
        col += LANES
    y = rope(proj[:, col:col + LANES], cb, sb, HEAD_DIM // 2)
    _split_heads(kb_ref, y, 0)
    col += LANES
    _split_heads(vb_ref, proj[:, col:col + LANES], 0)


def _even_proj(x, g, w, gq, gk, gm, ca, sa, cb, sb):
    const = lambda i: (0, 0)
    row = lambda i: (i, 0)
    hm = lambda n: pl.BlockSpec((n, PROJ_TM, HEAD_DIM), lambda i: (0, i, 0))
    hs = lambda n: jax.ShapeDtypeStruct((n, SEQ, HEAD_DIM), BF16)
    return pl.pallas_call(
        _even_proj_kernel,
        grid=(SEQ // PROJ_TM,),
        in_specs=[
            pl.BlockSpec((PROJ_TM, D_MODEL), row),
            pl.BlockSpec((1, D_MODEL), const),
            pl.BlockSpec((D_MODEL, EVEN_IN), const, pipeline_mode=pl.Buffered(1)),
            pl.BlockSpec((1, LANES), const),
            pl.BlockSpec((1, LANES), const),
            pl.BlockSpec((LANES, LANES), const),
            pl.BlockSpec((PROJ_TM, LANES), row),
            pl.BlockSpec((PROJ_TM, LANES), row),
            pl.BlockSpec((PROJ_TM, LANES), row),
            pl.BlockSpec((PROJ_TM, LANES), row),
        ],
        out_specs=[hm(A_HEADS), hm(A_KV), hm(A_KV), hm(B_HEADS), hm(B_KV), hm(B_KV)],
        out_shape=[hs(A_HEADS), hs(A_KV), hs(A_KV), hs(B_HEADS), hs(B_KV), hs(B_KV)],
        compiler_params=_cparams("arbitrary"),
        name="even_proj",
    )(x, g, w, gq, gk, gm, ca, sa, cb, sb)


A_TQ = 256
A_TK = 512


def _attn_a_kernel(q_ref, k_ref, v_ref, o_ref, m_sc, l_sc, acc_sc):
    rows = GROUP * A_TQ
    q = q_ref[...].reshape(rows, HEAD_DIM)
    m_sc[...] = jnp.full((rows, LANES), -jnp.inf, F32)
    l_sc[...] = jnp.zeros((rows, LANES), F32)
    acc_sc[...] = jnp.zeros((rows, HEAD_DIM), F32)

    def body(c, carry):
        off = pl.multiple_of(c * A_TK, A_TK)
        k = k_ref[pl.ds(off, A_TK), :]
        v = v_ref[pl.ds(off, A_TK), :]
        s = lax.dot_general(q, k, (((1,), (1,)), ((), ())), preferred_element_type=F32)
        m_prev = m_sc[...]
        m_new = jnp.maximum(m_prev, jnp.max(s, axis=1, keepdims=True))
        alpha = jnp.exp(m_prev - m_new)
        p = jnp.exp(s - m_new[:, :1])
        l_sc[...] = alpha * l_sc[...] + jnp.sum(p, axis=1, keepdims=True)
        acc_sc[...] = (acc_sc[...] * alpha[:, :HEAD_DIM]
                       + jnp.dot(p.astype(BF16), v, preferred_element_type=F32))
        m_sc[...] = m_new
        return carry

    lax.fori_loop(0, SEQ // A_TK, body, 0)
    o = acc_sc[...] / l_sc[...][:, :HEAD_DIM]
    o_ref[...] = o.reshape(GROUP, A_TQ, HEAD_DIM).astype(BF16)


def _attn_a(q, k, v):
    rows = GROUP * A_TQ
    return pl.pallas_call(
        _attn_a_kernel,
        grid=(A_KV, SEQ // A_TQ),
        in_specs=[
            pl.BlockSpec((GROUP, A_TQ, HEAD_DIM), lambda j, i: (j, i, 0)),
            pl.BlockSpec((None, SEQ, HEAD_DIM), lambda j, i: (j, 0, 0)),
            pl.BlockSpec((None, SEQ, HEAD_DIM), lambda j, i: (j, 0, 0)),
        ],
        out_specs=pl.BlockSpec((GROUP, A_TQ, HEAD_DIM), lambda j, i: (j, i, 0)),
        out_shape=jax.ShapeDtypeStruct((A_HEADS, SEQ, HEAD_DIM), BF16),
        scratch_shapes=[
            pltpu.VMEM((rows, LANES), F32),
            pltpu.VMEM((rows, LANES), F32),
            pltpu.VMEM((rows, HEAD_DIM), F32),
        ],
        compiler_params=_cparams("arbitrary", "arbitrary"),
        name="attn_a",
    )(q, k, v)


N_QBLK = SEQ // Q_BLOCK


def _attn_b_kernel(sink_ref, q_ref, k0_ref, k1_ref, k2_ref, v0_ref, v1_ref, v2_ref, o_ref):
    j = pl.program_id(0)
    n = pl.program_id(1)
    k = jnp.concatenate([k0_ref[...], k1_ref[...], k2_ref[...]], axis=0)
    v = jnp.concatenate([v0_ref[...], v1_ref[...], v2_ref[...]], axis=0)
    qi = lax.broadcasted_iota(jnp.int32, (Q_BLOCK, 3 * Q_BLOCK), 0)
    kj = lax.broadcasted_iota(jnp.int32, (Q_BLOCK, 3 * Q_BLOCK), 1)
    kpos = (n - 1) * Q_BLOCK + kj
    valid = (jnp.abs(qi + Q_BLOCK - kj) <= WINDOW) & (kpos >= 0) & (kpos < SEQ)
    for g in range(GROUP):
        sk = sink_ref[j * GROUP + g]
        s = lax.dot_general(q_ref[g], k, (((1,), (1,)), ((), ())), preferred_element_type=F32)
        s = jnp.where(valid, s, NEG_INF)
        m = jnp.maximum(jnp.max(s, axis=1, keepdims=True), sk)
        p = jnp.exp(s - m)
        denom = jnp.sum(p, axis=1, keepdims=True) + jnp.exp(sk - m)
        o = jnp.dot(p.astype(BF16), v, preferred_element_type=F32) / denom
        o_ref[g] = o.astype(BF16)


def _attn_b(sink, q, k, v):
    kv_spec = lambda d: pl.BlockSpec(
        (None, Q_BLOCK, HEAD_DIM),
        lambda j, n, s: (j, jnp.clip(n + d, 0, N_QBLK - 1), 0))
    q_spec = pl.BlockSpec((GROUP, Q_BLOCK, HEAD_DIM), lambda j, n, s: (j, n, 0))
    return pl.pallas_call(
        _attn_b_kernel,
        grid_spec=pltpu.PrefetchScalarGridSpec(
            num_scalar_prefetch=1,
            grid=(B_KV, N_QBLK),
            in_specs=[q_spec, kv_spec(-1), kv_spec(0), kv_spec(1),
                      kv_spec(-1), kv_spec(0), kv_spec(1)],
            out_specs=q_spec,
        ),
        out_shape=jax.ShapeDtypeStruct((B_HEADS, SEQ, HEAD_DIM), BF16),
        compiler_params=_cparams("arbitrary", "arbitrary"),
        name="attn_b",
    )(sink, q, k, k, k, v, v, v)


OUT_TM = 512


def _out_proj_kernel(x_ref, *refs, n_in):
    o_refs, w_ref, y_ref = refs[:n_in], refs[n_in], refs[n_in + 1]
    heads = [r[h] for r in o_refs for h in range(r.shape[0])]
    o = jnp.concatenate(heads, axis=-1)
    y_ref[...] = x_ref[...] + jnp.dot(o, w_ref[...], preferred_element_type=F32)


def _out_proj(x, outs, w):
    row = lambda i: (i, 0)
    in_specs = [pl.BlockSpec((OUT_TM, D_MODEL), row)]
    for o in outs:
        in_specs.append(pl.BlockSpec((o.shape[0], OUT_TM, HEAD_DIM), lambda i: (0, i, 0)))
    in_specs.append(pl.BlockSpec((D_MODEL, D_MODEL), lambda i: (0, 0), pipeline_mode=pl.Buffered(1)))
    return pl.pallas_call(
        functools.partial(_out_proj_kernel, n_in=len(outs)),
        grid=(SEQ // OUT_TM,),
        in_specs=in_specs,
        out_specs=pl.BlockSpec((OUT_TM, D_MODEL), row),
        out_shape=jax.ShapeDtypeStruct((SEQ, D_MODEL), F32),
        compiler_params=_cparams("arbitrary"),
        name="out_proj",
    )(x, *outs, w)


def _odd_proj_kernel(x_ref, g_ref, w_ref, q_ref, k_ref, v_ref):
    h = _rms(x_ref[...], g_ref[...]).astype(BF16)
    width = C_HEADS * HEAD_DIM
    for idx, (dst, scale) in enumerate(((q_ref, SCALE), (k_ref, None), (v_ref, None))):
        y = jnp.dot(h, w_ref[:, idx * width:(idx + 1) * width], preferred_element_type=F32)
        if scale is not None:
            y = y * scale
        _split_heads(dst, y, 0)


def _odd_proj(x, g, w):
    hm = pl.BlockSpec((C_HEADS, PROJ_TM, HEAD_DIM), lambda i: (0, i, 0))
    hs = jax.ShapeDtypeStruct((C_HEADS, SEQ, HEAD_DIM), BF16)
    return pl.pallas_call(
        _odd_proj_kernel,
        grid=(SEQ // PROJ_TM,),
        in_specs=[
            pl.BlockSpec((PROJ_TM, D_MODEL), lambda i: (i, 0)),
            pl.BlockSpec((1, D_MODEL), lambda i: (0, 0)),
            pl.BlockSpec((D_MODEL, 3 * C_HEADS * HEAD_DIM), lambda i: (0, 0),
                         pipeline_mode=pl.Buffered(1)),
        ],
        out_specs=[hm, hm, hm],
        out_shape=[hs, hs, hs],
        compiler_params=_cparams("arbitrary"),
        name="odd_proj",
    )(x, g, w)


KH = min(NA_KH, ROWS)
N_VARIANTS = KH


def _row_start(r):
    return jnp.clip(r - KH // 2, 0, ROWS - KH)


def _attn_c_kernel(q_ref, *refs):
    k_refs, v_refs = refs[:KH], refs[KH:2 * KH]
    bias_ref, o_ref = refs[2 * KH], refs[2 * KH + 1]
    for h in range(C_HEADS):
        k = jnp.concatenate([r[h] for r in k_refs], axis=0)
        v = jnp.concatenate([r[h] for r in v_refs], axis=0)
        s = lax.dot_general(q_ref[h], k, (((1,), (1,)), ((), ())), preferred_element_type=F32)
        s = s + bias_ref[h]
        m = jnp.max(s, axis=1, keepdims=True)
        p = jnp.exp(s - m)
        denom = jnp.sum(p, axis=1, keepdims=True)
        o = jnp.dot(p.astype(BF16), v, preferred_element_type=F32) / denom
        o_ref[h] = o.astype(BF16)


def _attn_c(q, k, v, bias):
    blk = (C_HEADS, GRID_W, HEAD_DIM)
    q_spec = pl.BlockSpec(blk, lambda r: (0, r, 0))
    kv_spec = lambda a: pl.BlockSpec(blk, lambda r: (0, _row_start(r) + a, 0))
    bias_spec = pl.BlockSpec((None, C_HEADS, GRID_W, KH * GRID_W),
                             lambda r: (_row_start(r) - r + (KH - 1), 0, 0, 0))
    return pl.pallas_call(
        _attn_c_kernel,
        grid=(ROWS,),
        in_specs=[q_spec] + [kv_spec(a) for a in range(KH)] * 2 + [bias_spec],
        out_specs=q_spec,
        out_shape=jax.ShapeDtypeStruct((C_HEADS, SEQ, HEAD_DIM), BF16),
        compiler_params=_cparams("arbitrary"),
        name="attn_c",
    )(q, *([k] * KH), *([v] * KH), bias)


def _neighbourhood_bias(rel_bias):
    cols = np.arange(GRID_W)
    col_start = np.clip(cols - NA_KW // 2, 0, GRID_W - NA_KW)
    kc = np.arange(GRID_W)
    in_win = (kc[None, :] >= col_start[:, None]) & (kc[None, :] < col_start[:, None] + NA_KW)
    col_idx = np.clip(kc[None, :] - cols[:, None] + (NA_KW - 1), 0, 2 * NA_KW - 2)
    row_idx = (np.arange(N_VARIANTS)[:, None] - (KH - 1)) + np.arange(KH)[None, :] + (NA_KH - 1)
    b = rel_bias[:, row_idx][:, :, :, col_idx]
    b = jnp.where(in_win[None, None, None], b, NEG_INF)
    b = b.transpose(1, 0, 3, 2, 4)
    return b.reshape(N_VARIANTS, C_HEADS, GRID_W, KH * GRID_W).astype(F32)


def _rope_tables():
    pos = jnp.arange(SEQ)

    def cos_sin(p, dim):
        inv = ROPE_THETA ** (-jnp.arange(0, dim, 2, dtype=F32) / dim)
        ang = p.astype(F32)[:, None] * inv[None, :]
        return jnp.cos(ang), jnp.sin(ang)

    c1, s1 = cos_sin(pos, HEAD_DIM)
    cr, sr = cos_sin(pos // GRID_W, HEAD_DIM // 2)
    cc, sc = cos_sin(pos % GRID_W, HEAD_DIM // 2)
    reps = LANES // HEAD_DIM
    tile = lambda parts: jnp.tile(jnp.concatenate(parts, axis=-1), (1, reps))
    return (tile([cr, cr, cc, cc]), tile([-sr, sr, -sc, sc]),
            tile([c1, c1]), tile([-s1, s1]))


def kernel(x, ffn1_norm, ffn1_w_gate, ffn1_w_up, ffn1_w_down, mix_norm, ffn2_norm, ffn2_w_gate,
           ffn2_w_up, ffn2_w_down, even_w_in, a_q_norm, a_k_norm, b_sink, even_w_out, odd_w_qkv,
           c_rel_bias, odd_w_out, final_norm):
    assert x.shape == (1, SEQ, D_MODEL)
    xs = x.reshape(SEQ, D_MODEL)
    ca, sa, cb, sb = _rope_tables()
    reps = LANES // HEAD_DIM
    group_mean = jnp.asarray(
        np.kron(np.eye(reps), np.full((HEAD_DIM, HEAD_DIM), 1.0 / HEAD_DIM)), BF16)
    fg = final_norm.reshape(1, D_MODEL)
    row = lambda v: v.reshape(1, -1)

    for layer in range(DEPTH):
        i = layer // 2
        xs = _ffn(xs, row(ffn1_norm[layer]), ffn1_w_gate[layer].astype(BF16),
                  ffn1_w_up[layer].astype(BF16), ffn1_w_down[layer].astype(BF16), fg, False)
        g = row(mix_norm[layer])
        if layer % 2 == 0:
            qa, ka, va, qb, kb, vb = _even_proj(
                xs, g, even_w_in[i].astype(BF16),
                row(jnp.tile(a_q_norm[i], reps)), row(jnp.tile(a_k_norm[i], reps)),
                group_mean, ca, sa, cb, sb)
            oa = _attn_a(qa, ka, va)
            ob = _attn_b(b_sink[i], qb, kb, vb)
            xs = _out_proj(xs, [oa, ob], even_w_out[i].astype(BF16))
        else:
            q, k, v = _odd_proj(xs, g, odd_w_qkv[i].astype(BF16))
            oc = _attn_c(q, k, v, _neighbourhood_bias(c_rel_bias[i]))
            xs = _out_proj(xs, [oc], odd_w_out[i].astype(BF16))
        xs = _ffn(xs, row(ffn2_norm[layer]), ffn2_w_gate[layer].astype(BF16),
                  ffn2_w_up[layer].astype(BF16), ffn2_w_down[layer].astype(BF16), fg,
                  layer == DEPTH - 1)
    return xs.reshape(1, SEQ, D_MODEL)
```

```python
import functools

import jax
import jax.numpy as jnp
import numpy as np
from jax import lax
from jax.experimental import pallas as pl
from jax.experimental.pallas import tpu as pltpu

D_MODEL = 1024
SEQ = 16384
DEPTH = 2
HEAD_DIM = 64
A_HEADS = 8
A_KV = 2
B_HEADS = 8
B_KV = 2
C_HEADS = 16
D_FF = 2816
GRID_W = 64
ROWS = SEQ // GRID_W
Q_BLOCK = 128
WINDOW = 128
NA_KH = 8
NA_KW = 16
ROPE_THETA = 10000.0
EPS = 1e-6
EVEN_IN = (A_HEADS + 2 * A_KV + B_HEADS + 2 * B_KV) * HEAD_DIM
NEG_INF = -1e30
SCALE = HEAD_DIM ** -0.5

LANES = 128
GROUP = A_HEADS // A_KV

F32 = jnp.float32
BF16 = jnp.bfloat16

VMEM_LIMIT = 56 * 1024 * 1024


def _cparams(*sem):
    return pltpu.CompilerParams(dimension_semantics=sem, vmem_limit_bytes=VMEM_LIMIT)


def _rms(x, g):
    ms = jnp.mean(x * x, axis=-1, keepdims=True)
    return x * lax.rsqrt(ms + EPS) * g


FFN_TM = 512
FFN_CHUNK = D_FF // 2


def _ffn_kernel(x_ref, g_ref, wg_ref, wu_ref, wd_ref, fg_ref, o_ref, *, final_norm):
    x = x_ref[...]
    h = _rms(x, g_ref[...]).astype(BF16)
    acc = jnp.zeros_like(x)
    for c in range(D_FF // FFN_CHUNK):
        sl = slice(c * FFN_CHUNK, (c + 1) * FFN_CHUNK)
        gate = jnp.dot(h, wg_ref[:, sl], preferred_element_type=F32)
        up = jnp.dot(h, wu_ref[:, sl], preferred_element_type=F32)
        act = (gate * jax.nn.sigmoid(gate) * up).astype(BF16)
        acc = acc + jnp.dot(act, wd_ref[sl, :], preferred_element_type=F32)
    y = x + 0.5 * acc
    if final_norm:
        y = _rms(y, fg_ref[...])
    o_ref[...] = y


def _ffn(x, g, wg, wu, wd, fg, final_norm):
    const = lambda i: (0, 0)
    return pl.pallas_call(
        functools.partial(_ffn_kernel, final_norm=final_norm),
        grid=(SEQ // FFN_TM,),
        in_specs=[
            pl.BlockSpec((FFN_TM, D_MODEL), lambda i: (i, 0)),
            pl.BlockSpec((1, D_MODEL), const),
            pl.BlockSpec((D_MODEL, D_FF), const, pipeline_mode=pl.Buffered(1)),
            pl.BlockSpec((D_MODEL, D_FF), const, pipeline_mode=pl.Buffered(1)),
            pl.BlockSpec((D_FF, D_MODEL), const, pipeline_mode=pl.Buffered(1)),
            pl.BlockSpec((1, D_MODEL), const),
        ],
        out_specs=pl.BlockSpec((FFN_TM, D_MODEL), lambda i: (i, 0)),
        out_shape=jax.ShapeDtypeStruct((SEQ, D_MODEL), F32),
        compiler_params=_cparams("arbitrary"),
        name="ffn",
    )(x, g, wg, wu, wd, fg)


PROJ_TM = 512


def _split_heads(dst_ref, y, first_head):
    for t in range(y.shape[1] // HEAD_DIM):
        dst_ref[first_head + t] = y[:, t * HEAD_DIM:(t + 1) * HEAD_DIM].astype(BF16)


def _split_heads_t(dst_ref, y, first_head):
    yt = y.T
    for t in range(y.shape[1] // HEAD_DIM):
        dst_ref[first_head + t] = yt[t * HEAD_DIM:(t + 1) * HEAD_DIM, :].astype(BF16)


def _even_proj_kernel(x_ref, g_ref, w_ref, gq_ref, gk_ref, gm_ref, ca_ref, sa_ref, cb_ref, sb_ref,
                      qa_ref, ka_ref, va_ref, qb_ref, kb_ref, vb_ref):
    h = _rms(x_ref[...], g_ref[...]).astype(BF16)
    proj = jnp.dot(h, w_ref[...], preferred_element_type=F32)
    lane = lax.broadcasted_iota(jnp.int32, (PROJ_TM, LANES), 1)
    gm = gm_ref[...]

    def head_norm(y, gain):
        sq = y * y
        hi = sq.astype(BF16)
        lo = (sq - hi.astype(F32)).astype(BF16)
        ms = (jnp.dot(hi, gm, preferred_element_type=F32)
              + jnp.dot(lo, gm, preferred_element_type=F32))
        return y * lax.rsqrt(ms + EPS) * gain

    def rope(y, c, s, half):
        fwd = pltpu.roll(y, LANES - half, 1)
        bwd = pltpu.roll(y, half, 1)
        partner = jnp.where((lane % (2 * half)) < half, fwd, bwd)
        return y * c + partner * s

    ca, sa, cb, sb = ca_ref[...], sa_ref[...], cb_ref[...], sb_ref[...]
    col = 0
    for j in range(A_HEADS // 2):
        y = rope(head_norm(proj[:, col:col + LANES], gq_ref[...]), ca, sa, HEAD_DIM // 4) * SCALE
        _split_heads_t(qa_ref, y, 2 * j)
        col += LANES
    y = rope(head_norm(proj[:, col:col + LANES], gk_ref[...]), ca, sa, HEAD_DIM // 4)
    _split_heads(ka_ref, y, 0)
    col += LANES
    _split_heads_t(va_ref, proj[:, col:col + LANES], 0)
    col += LANES
    for j in range(B_HEADS // 2):
        y = rope(proj[:, col:col + LANES], cb, sb, HEAD_DIM // 2) * SCALE
        _split_heads(qb_ref, y, 2 * j)
        col += LANES
    y = rope(proj[:, col:col + LANES], cb, sb, HEAD_DIM // 2)
    _split_heads(kb_ref, y, 0)
    col += LANES
    _split_heads(vb_ref, proj[:, col:col + LANES], 0)


def _even_proj(x, g, w, gq, gk, gm, ca, sa, cb, sb):
    const = lambda i: (0, 0)
    row = lambda i: (i, 0)
    hm = lambda n: pl.BlockSpec((n, PROJ_TM, HEAD_DIM), lambda i: (0, i, 0))
    hs = lambda n: jax.ShapeDtypeStruct((n, SEQ, HEAD_DIM), BF16)
    hmt = lambda n: pl.BlockSpec((n, HEAD_DIM, PROJ_TM), lambda i: (0, 0, i))
    hst = lambda n: jax.ShapeDtypeStruct((n, HEAD_DIM, SEQ), BF16)
    return pl.pallas_call(
        _even_proj_kernel,
        grid=(SEQ // PROJ_TM,),
        in_specs=[
            pl.BlockSpec((PROJ_TM, D_MODEL), row),
            pl.BlockSpec((1, D_MODEL), const),
            pl.BlockSpec((D_MODEL, EVEN_IN), const, pipeline_mode=pl.Buffered(1)),
            pl.BlockSpec((1, LANES), const),
            pl.BlockSpec((1, LANES), const),
            pl.BlockSpec((LANES, LANES), const),
            pl.BlockSpec((PROJ_TM, LANES), row),
            pl.BlockSpec((PROJ_TM, LANES), row),
            pl.BlockSpec((PROJ_TM, LANES), row),
            pl.BlockSpec((PROJ_TM, LANES), row),
        ],
        out_specs=[hmt(A_HEADS), hm(A_KV), hmt(A_KV), hm(B_HEADS), hm(B_KV), hm(B_KV)],
        out_shape=[hst(A_HEADS), hs(A_KV), hst(A_KV), hs(B_HEADS), hs(B_KV), hs(B_KV)],
        compiler_params=_cparams("arbitrary"),
        name="even_proj",
    )(x, g, w, gq, gk, gm, ca, sa, cb, sb)


A_TQ = 256
A_TK = 512


def _attn_a_kernel(qt_ref, k_ref, vt_ref, o_ref, qt_sc, acc_sc):
    cols = GROUP * A_TQ
    for g in range(GROUP):
        qt_sc[:, g * A_TQ:(g + 1) * A_TQ] = qt_ref[g]
    acc_sc[...] = jnp.zeros((HEAD_DIM, cols), F32)

    def body(c, carry):
        m_prev, l_prev = carry
        off = pl.multiple_of(c * A_TK, A_TK)
        k = k_ref[pl.ds(off, A_TK), :]
        st = jnp.dot(k, qt_sc[...], preferred_element_type=F32)
        m_new = jnp.maximum(m_prev, jnp.max(st, axis=0, keepdims=True))
        alpha = jnp.exp(m_prev - m_new)
        pt = jnp.exp(st - m_new)
        l_new = alpha * l_prev + jnp.sum(pt, axis=0, keepdims=True)
        vt = vt_ref[:, pl.ds(off, A_TK)]
        acc_sc[...] = acc_sc[...] * alpha + jnp.dot(vt, pt.astype(BF16),
                                                    preferred_element_type=F32)
        return m_new, l_new

    init = (jnp.full((1, cols), -jnp.inf, F32), jnp.zeros((1, cols), F32))
    _, l = lax.fori_loop(0, SEQ // A_TK, body, init)
    ot = acc_sc[...] / l
    for g in range(GROUP):
        o_ref[g] = ot[:, g * A_TQ:(g + 1) * A_TQ].T.astype(BF16)


def _attn_a(qt, k, vt):
    cols = GROUP * A_TQ
    return pl.pallas_call(
        _attn_a_kernel,
        grid=(A_KV, SEQ // A_TQ),
        in_specs=[
            pl.BlockSpec((GROUP, HEAD_DIM, A_TQ), lambda j, i: (j, 0, i)),
            pl.BlockSpec((None, SEQ, HEAD_DIM), lambda j, i: (j, 0, 0)),
            pl.BlockSpec((None, HEAD_DIM, SEQ), lambda j, i: (j, 0, 0)),
        ],
        out_specs=pl.BlockSpec((GROUP, A_TQ, HEAD_DIM), lambda j, i: (j, i, 0)),
        out_shape=jax.ShapeDtypeStruct((A_HEADS, SEQ, HEAD_DIM), BF16),
        scratch_shapes=[
            pltpu.VMEM((HEAD_DIM, cols), BF16),
            pltpu.VMEM((HEAD_DIM, cols), F32),
        ],
        compiler_params=_cparams("arbitrary", "arbitrary"),
        name="attn_a",
    )(qt, k, vt)


N_QBLK = SEQ // Q_BLOCK


def _attn_b_kernel(sink_ref, q_ref, k0_ref, k1_ref, k2_ref, v0_ref, v1_ref, v2_ref, o_ref):
    j = pl.program_id(0)
    n = pl.program_id(1)
    k = jnp.concatenate([k0_ref[...], k1_ref[...], k2_ref[...]], axis=0)
    v = jnp.concatenate([v0_ref[...], v1_ref[...], v2_ref[...]], axis=0)
    qi = lax.broadcasted_iota(jnp.int32, (Q_BLOCK, 3 * Q_BLOCK), 0)
    kj = lax.broadcasted_iota(jnp.int32, (Q_BLOCK, 3 * Q_BLOCK), 1)
    kpos = (n - 1) * Q_BLOCK + kj
    valid = (jnp.abs(qi + Q_BLOCK - kj) <= WINDOW) & (kpos >= 0) & (kpos < SEQ)
    for g in range(GROUP):
        sk = sink_ref[j * GROUP + g]
        s = lax.dot_general(q_ref[g], k, (((1,), (1,)), ((), ())), preferred_element_type=F32)
        s = jnp.where(valid, s, NEG_INF)
        m = jnp.maximum(jnp.max(s, axis=1, keepdims=True), sk)
        p = jnp.exp(s - m)
        denom = jnp.sum(p, axis=1, keepdims=True) + jnp.exp(sk - m)
        o = jnp.dot(p.astype(BF16), v, preferred_element_type=F32) / denom
        o_ref[g] = o.astype(BF16)


def _attn_b(sink, q, k, v):
    kv_spec = lambda d: pl.BlockSpec(
        (None, Q_BLOCK, HEAD_DIM),
        lambda j, n, s: (j, jnp.clip(n + d, 0, N_QBLK - 1), 0))
    q_spec = pl.BlockSpec((GROUP, Q_BLOCK, HEAD_DIM), lambda j, n, s: (j, n, 0))
    return pl.pallas_call(
        _attn_b_kernel,
        grid_spec=pltpu.PrefetchScalarGridSpec(
            num_scalar_prefetch=1,
            grid=(B_KV, N_QBLK),
            in_specs=[q_spec, kv_spec(-1), kv_spec(0), kv_spec(1),
                      kv_spec(-1), kv_spec(0), kv_spec(1)],
            out_specs=q_spec,
        ),
        out_shape=jax.ShapeDtypeStruct((B_HEADS, SEQ, HEAD_DIM), BF16),
        compiler_params=_cparams("arbitrary", "arbitrary"),
        name="attn_b",
    )(sink, q, k, k, k, v, v, v)


OUT_TM = 512


def _out_proj_kernel(x_ref, *refs, n_in):
    o_refs, w_ref, y_ref = refs[:n_in], refs[n_in], refs[n_in + 1]
    heads = [r[h] for r in o_refs for h in range(r.shape[0])]
    o = jnp.concatenate(heads, axis=-1)
    y_ref[...] = x_ref[...] + jnp.dot(o, w_ref[...], preferred_element_type=F32)


def _out_proj(x, outs, w):
    row = lambda i: (i, 0)
    in_specs = [pl.BlockSpec((OUT_TM, D_MODEL), row)]
    for o in outs:
        in_specs.append(pl.BlockSpec((o.shape[0], OUT_TM, HEAD_DIM), lambda i: (0, i, 0)))
    in_specs.append(pl.BlockSpec((D_MODEL, D_MODEL), lambda i: (0, 0), pipeline_mode=pl.Buffered(1)))
    return pl.pallas_call(
        functools.partial(_out_proj_kernel, n_in=len(outs)),
        grid=(SEQ // OUT_TM,),
        in_specs=in_specs,
        out_specs=pl.BlockSpec((OUT_TM, D_MODEL), row),
        out_shape=jax.ShapeDtypeStruct((SEQ, D_MODEL), F32),
        compiler_params=_cparams("arbitrary"),
        name="out_proj",
    )(x, *outs, w)


def _odd_proj_kernel(x_ref, g_ref, w_ref, q_ref, k_ref, v_ref):
    h = _rms(x_ref[...], g_ref[...]).astype(BF16)
    width = C_HEADS * HEAD_DIM
    for idx, (dst, scale) in enumerate(((q_ref, SCALE), (k_ref, None), (v_ref, None))):
        y = jnp.dot(h, w_ref[:, idx * width:(idx + 1) * width], preferred_element_type=F32)
        if scale is not None:
            y = y * scale
        _split_heads(dst, y, 0)


def _odd_proj(x, g, w):
    hm = pl.BlockSpec((C_HEADS, PROJ_TM, HEAD_DIM), lambda i: (0, i, 0))
    hs = jax.ShapeDtypeStruct((C_HEADS, SEQ, HEAD_DIM), BF16)
    return pl.pallas_call(
        _odd_proj_kernel,
        grid=(SEQ // PROJ_TM,),
        in_specs=[
            pl.BlockSpec((PROJ_TM, D_MODEL), lambda i: (i, 0)),
            pl.BlockSpec((1, D_MODEL), lambda i: (0, 0)),
            pl.BlockSpec((D_MODEL, 3 * C_HEADS * HEAD_DIM), lambda i: (0, 0),
                         pipeline_mode=pl.Buffered(1)),
        ],
        out_specs=[hm, hm, hm],
        out_shape=[hs, hs, hs],
        compiler_params=_cparams("arbitrary"),
        name="odd_proj",
    )(x, g, w)


KH = min(NA_KH, ROWS)
N_VARIANTS = KH


def _row_start(r):
    return jnp.clip(r - KH // 2, 0, ROWS - KH)


def _attn_c_kernel(q_ref, *refs):
    k_refs, v_refs = refs[:KH], refs[KH:2 * KH]
    bias_ref, o_ref = refs[2 * KH], refs[2 * KH + 1]
    for h in range(C_HEADS):
        k = jnp.concatenate([r[h] for r in k_refs], axis=0)
        v = jnp.concatenate([r[h] for r in v_refs], axis=0)
        s = lax.dot_general(q_ref[h], k, (((1,), (1,)), ((), ())), preferred_element_type=F32)
        s = s + bias_ref[h]
        m = jnp.max(s, axis=1, keepdims=True)
        p = jnp.exp(s - m)
        denom = jnp.sum(p, axis=1, keepdims=True)
        o = jnp.dot(p.astype(BF16), v, preferred_element_type=F32) / denom
        o_ref[h] = o.astype(BF16)


def _attn_c(q, k, v, bias):
    blk = (C_HEADS, GRID_W, HEAD_DIM)
    q_spec = pl.BlockSpec(blk, lambda r: (0, r, 0))
    kv_spec = lambda a: pl.BlockSpec(blk, lambda r: (0, _row_start(r) + a, 0))
    bias_spec = pl.BlockSpec((None, C_HEADS, GRID_W, KH * GRID_W),
                             lambda r: (_row_start(r) - r + (KH - 1), 0, 0, 0))
    return pl.pallas_call(
        _attn_c_kernel,
        grid=(ROWS,),
        in_specs=[q_spec] + [kv_spec(a) for a in range(KH)] * 2 + [bias_spec],
        out_specs=q_spec,
        out_shape=jax.ShapeDtypeStruct((C_HEADS, SEQ, HEAD_DIM), BF16),
        compiler_params=_cparams("arbitrary"),
        name="attn_c",
    )(q, *([k] * KH), *([v] * KH), bias)


def _neighbourhood_bias(rel_bias):
    cols = np.arange(GRID_W)
    col_start = np.clip(cols - NA_KW // 2, 0, GRID_W - NA_KW)
    kc = np.arange(GRID_W)
    in_win = (kc[None, :] >= col_start[:, None]) & (kc[None, :] < col_start[:, None] + NA_KW)
    col_idx = np.clip(kc[None, :] - cols[:, None] + (NA_KW - 1), 0, 2 * NA_KW - 2)
    row_idx = (np.arange(N_VARIANTS)[:, None] - (KH - 1)) + np.arange(KH)[None, :] + (NA_KH - 1)
    b = rel_bias[:, row_idx][:, :, :, col_idx]
    b = jnp.where(in_win[None, None, None], b, NEG_INF)
    b = b.transpose(1, 0, 3, 2, 4)
    return b.reshape(N_VARIANTS, C_HEADS, GRID_W, KH * GRID_W).astype(F32)


def _rope_tables():
    pos = jnp.arange(SEQ)

    def cos_sin(p, dim):
        inv = ROPE_THETA ** (-jnp.arange(0, dim, 2, dtype=F32) / dim)
        ang = p.astype(F32)[:, None] * inv[None, :]
        return jnp.cos(ang), jnp.sin(ang)

    c1, s1 = cos_sin(pos, HEAD_DIM)
    cr, sr = cos_sin(pos // GRID_W, HEAD_DIM // 2)
    cc, sc = cos_sin(pos % GRID_W, HEAD_DIM // 2)
    reps = LANES // HEAD_DIM
    tile = lambda parts: jnp.tile(jnp.concatenate(parts, axis=-1), (1, reps))
    return (tile([cr, cr, cc, cc]), tile([-sr, sr, -sc, sc]),
            tile([c1, c1]), tile([-s1, s1]))


def kernel(x, ffn1_norm, ffn1_w_gate, ffn1_w_up, ffn1_w_down, mix_norm, ffn2_norm, ffn2_w_gate,
           ffn2_w_up, ffn2_w_down, even_w_in, a_q_norm, a_k_norm, b_sink, even_w_out, odd_w_qkv,
           c_rel_bias, odd_w_out, final_norm):
    assert x.shape == (1, SEQ, D_MODEL)
    xs = x.reshape(SEQ, D_MODEL)
    ca, sa, cb, sb = _rope_tables()
    reps = LANES // HEAD_DIM
    group_mean = jnp.asarray(
        np.kron(np.eye(reps), np.full((HEAD_DIM, HEAD_DIM), 1.0 / HEAD_DIM)), BF16)
    fg = final_norm.reshape(1, D_MODEL)
    row = lambda v: v.reshape(1, -1)

    for layer in range(DEPTH):
        i = layer // 2
        xs = _ffn(xs, row(ffn1_norm[layer]), ffn1_w_gate[layer].astype(BF16),
                  ffn1_w_up[layer].astype(BF16), ffn1_w_down[layer].astype(BF16), fg, False)
        g = row(mix_norm[layer])
        if layer % 2 == 0:
            qa, ka, va, qb, kb, vb = _even_proj(
                xs, g, even_w_in[i].astype(BF16),
                row(jnp.tile(a_q_norm[i], reps)), row(jnp.tile(a_k_norm[i], reps)),
                group_mean, ca, sa, cb, sb)
            oa = _attn_a(qa, ka, va)
            ob = _attn_b(b_sink[i], qb, kb, vb)
            xs = _out_proj(xs, [oa, ob], even_w_out[i].astype(BF16))
        else:
            q, k, v = _odd_proj(xs, g, odd_w_qkv[i].astype(BF16))
            oc = _attn_c(q, k, v, _neighbourhood_bias(c_rel_bias[i]))
            xs = _out_proj(xs, [oc], odd_w_out[i].astype(BF16))
        xs = _ffn(xs, row(ffn2_norm[layer]), ffn2_w_gate[layer].astype(BF16),
                  ffn2_w_up[layer].astype(BF16), ffn2_w_down[layer].astype(BF16), fg,
                  layer == DEPTH - 1)
    return xs.reshape(1, SEQ, D_MODEL)
```

```python
import functools

import jax
import jax.numpy as jnp
import numpy as np
from jax import lax
from jax.experimental import pallas as pl
from jax.experimental.pallas import tpu as pltpu

D_MODEL = 1024
SEQ = 16384
DEPTH = 2
HEAD_DIM = 64
A_HEADS = 8
A_KV = 2
B_HEADS = 8
B_KV = 2
C_HEADS = 16
D_FF = 2816
GRID_W = 64
ROWS = SEQ // GRID_W
Q_BLOCK = 128
WINDOW = 128
NA_KH = 8
NA_KW = 16
ROPE_THETA = 10000.0
EPS = 1e-6
EVEN_IN = (A_HEADS + 2 * A_KV + B_HEADS + 2 * B_KV) * HEAD_DIM
NEG_INF = -1e30
LOG2E = float(np.log2(np.e))
QSCALE = HEAD_DIM ** -0.5 * LOG2E

LANES = 128
MXU_N = 256
GROUP = A_HEADS // A_KV
V_ROWS = HEAD_DIM + 16

F32 = jnp.float32
BF16 = jnp.bfloat16

VMEM_LIMIT = 56 * 1024 * 1024


def _cparams(*sem):
    return pltpu.CompilerParams(dimension_semantics=sem, vmem_limit_bytes=VMEM_LIMIT)


def _rms(x, g):
    ms = jnp.mean(x * x, axis=-1, keepdims=True)
    return x * lax.rsqrt(ms + EPS) * g


FFN_TM = 512
FFN_CHUNK = D_FF // 2


def _ffn_kernel(x_ref, g_ref, wg_ref, wu_ref, wd_ref, fg_ref, o_ref, *, final_norm):
    x = x_ref[...]
    h = _rms(x, g_ref[...]).astype(BF16)
    acc = jnp.zeros_like(x)
    for c in range(D_FF // FFN_CHUNK):
        sl = slice(c * FFN_CHUNK, (c + 1) * FFN_CHUNK)
        gate = jnp.dot(h, wg_ref[:, sl], preferred_element_type=F32)
        up = jnp.dot(h, wu_ref[:, sl], preferred_element_type=F32)
        act = (gate * jax.nn.sigmoid(gate) * up).astype(BF16)
        acc = acc + jnp.dot(act, wd_ref[sl, :], preferred_element_type=F32)
    y = x + 0.5 * acc
    if final_norm:
        y = _rms(y, fg_ref[...])
    o_ref[...] = y


def _ffn(x, g, wg, wu, wd, fg, final_norm):
    const = lambda i: (0, 0)
    return pl.pallas_call(
        functools.partial(_ffn_kernel, final_norm=final_norm),
        grid=(SEQ // FFN_TM,),
        in_specs=[
            pl.BlockSpec((FFN_TM, D_MODEL), lambda i: (i, 0)),
            pl.BlockSpec((1, D_MODEL), const),
            pl.BlockSpec((D_MODEL, D_FF), const, pipeline_mode=pl.Buffered(1)),
            pl.BlockSpec((D_MODEL, D_FF), const, pipeline_mode=pl.Buffered(1)),
            pl.BlockSpec((D_FF, D_MODEL), const, pipeline_mode=pl.Buffered(1)),
            pl.BlockSpec((1, D_MODEL), const),
        ],
        out_specs=pl.BlockSpec((FFN_TM, D_MODEL), lambda i: (i, 0)),
        out_shape=jax.ShapeDtypeStruct((SEQ, D_MODEL), F32),
        compiler_params=_cparams("arbitrary"),
        name="ffn",
    )(x, g, wg, wu, wd, fg)


PROJ_TM = 512


def _split_heads(dst_ref, y, first_head):
    for t in range(y.shape[1] // HEAD_DIM):
        dst_ref[first_head + t] = y[:, t * HEAD_DIM:(t + 1) * HEAD_DIM].astype(BF16)


def _split_heads_t(dst_ref, y, first_head, ones_row=False):
    yt = y.T
    tm = y.shape[0]
    for t in range(y.shape[1] // HEAD_DIM):
        dst_ref[first_head + t, 0:HEAD_DIM, :] = yt[t * HEAD_DIM:(t + 1) * HEAD_DIM, :].astype(BF16)
        if ones_row:
            r = lax.broadcasted_iota(jnp.int32, (V_ROWS - HEAD_DIM, tm), 0)
            dst_ref[first_head + t, HEAD_DIM:V_ROWS, :] = jnp.where(r == 0, 1.0, 0.0).astype(BF16)


def _even_proj_kernel(x_ref, g_ref, w_ref, gq_ref, gk_ref, gm_ref, ca_ref, sa_ref, cb_ref, sb_ref,
                      qa_ref, ka_ref, va_ref, qb_ref, kb_ref, vb_ref):
    h = _rms(x_ref[...], g_ref[...]).astype(BF16)
    proj = jnp.dot(h, w_ref[...], preferred_element_type=F32)
    lane = lax.broadcasted_iota(jnp.int32, (PROJ_TM, LANES), 1)
    gm = gm_ref[...]

    def head_norm(y, gain):
        sq = y * y
        hi = sq.astype(BF16)
        lo = (sq - hi.astype(F32)).astype(BF16)
        ms = (jnp.dot(hi, gm, preferred_element_type=F32)
              + jnp.dot(lo, gm, preferred_element_type=F32))
        return y * lax.rsqrt(ms + EPS) * gain

    def rope(y, c, s, half):
        fwd = pltpu.roll(y, LANES - half, 1)
        bwd = pltpu.roll(y, half, 1)
        partner = jnp.where((lane % (2 * half)) < half, fwd, bwd)
        return y * c + partner * s

    ca, sa, cb, sb = ca_ref[...], sa_ref[...], cb_ref[...], sb_ref[...]
    col = 0
    for j in range(A_HEADS // 2):
        y = rope(head_norm(proj[:, col:col + LANES], gq_ref[...]), ca, sa, HEAD_DIM // 4) * QSCALE
        _split_heads_t(qa_ref, y, 2 * j)
        col += LANES
    y = rope(head_norm(proj[:, col:col + LANES], gk_ref[...]), ca, sa, HEAD_DIM // 4)
    _split_heads(ka_ref, y, 0)
    col += LANES
    _split_heads_t(va_ref, proj[:, col:col + LANES], 0, ones_row=True)
    col += LANES
    for j in range(B_HEADS // 2):
        y = rope(proj[:, col:col + LANES], cb, sb, HEAD_DIM // 2) * QSCALE
        _split_heads_t(qb_ref, y, 2 * j)
        col += LANES
    y = rope(proj[:, col:col + LANES], cb, sb, HEAD_DIM // 2)
    _split_heads(kb_ref, y, 0)
    col += LANES
    _split_heads_t(vb_ref, proj[:, col:col + LANES], 0, ones_row=True)


def _even_proj(x, g, w, gq, gk, gm, ca, sa, cb, sb):
    const = lambda i: (0, 0)
    row = lambda i: (i, 0)
    k_spec = pl.BlockSpec((A_KV, PROJ_TM, HEAD_DIM), lambda i: (0, i, 0))
    k_shape = jax.ShapeDtypeStruct((A_KV, SEQ, HEAD_DIM), BF16)
    t_spec = lambda n, rows: pl.BlockSpec((n, rows, PROJ_TM), lambda i: (0, 0, i))
    t_shape = lambda n, rows: jax.ShapeDtypeStruct((n, rows, SEQ), BF16)
    return pl.pallas_call(
        _even_proj_kernel,
        grid=(SEQ // PROJ_TM,),
        in_specs=[
            pl.BlockSpec((PROJ_TM, D_MODEL), row),
            pl.BlockSpec((1, D_MODEL), const),
            pl.BlockSpec((D_MODEL, EVEN_IN), const, pipeline_mode=pl.Buffered(1)),
            pl.BlockSpec((1, LANES), const),
            pl.BlockSpec((1, LANES), const),
            pl.BlockSpec((LANES, LANES), const),
            pl.BlockSpec((PROJ_TM, LANES), row),
            pl.BlockSpec((PROJ_TM, LANES), row),
            pl.BlockSpec((PROJ_TM, LANES), row),
            pl.BlockSpec((PROJ_TM, LANES), row),
        ],
        out_specs=[t_spec(A_HEADS, HEAD_DIM), k_spec, t_spec(A_KV, V_ROWS),
                   t_spec(B_HEADS, HEAD_DIM), k_spec, t_spec(B_KV, V_ROWS)],
        out_shape=[t_shape(A_HEADS, HEAD_DIM), k_shape, t_shape(A_KV, V_ROWS),
                   t_shape(B_HEADS, HEAD_DIM), k_shape, t_shape(B_KV, V_ROWS)],
        compiler_params=_cparams("arbitrary"),
        name="even_proj",
    )(x, g, w, gq, gk, gm, ca, sa, cb, sb)


A_TQ = 512
A_TK = 512
A_UNROLL = 4
A_MAX_LEAD = 64.0


def _attn_a_kernel(qt_ref, k_ref, vt_ref, o_ref, qt_sc, acc_sc):
    cols = GROUP * A_TQ
    for g in range(GROUP):
        qt_sc[:, g * A_TQ:(g + 1) * A_TQ] = qt_ref[g]

    def scores(off):
        return jnp.dot(k_ref[pl.ds(off, A_TK), :], qt_sc[...], preferred_element_type=F32)

    def weighted_values(off, pt):
        return jnp.dot(vt_ref[:, pl.ds(off, A_TK)], pt, preferred_element_type=F32)

    def finish():
        acc = acc_sc[...]
        ot = acc[:HEAD_DIM] / acc[HEAD_DIM:HEAD_DIM + 1]
        for g in range(GROUP):
            o_ref[g] = ot[:, g * A_TQ:(g + 1) * A_TQ].T.astype(BF16)

    m_ref = jnp.max(scores(0), axis=0, keepdims=True)
    acc_sc[...] = jnp.zeros((V_ROWS, cols), F32)

    def fast_body(c, m_run):
        offs = [pl.multiple_of((c * A_UNROLL + u) * A_TK, A_TK) for u in range(A_UNROLL)]
        st_next = scores(offs[0])
        for u in range(A_UNROLL):
            st = st_next
            if u + 1 < A_UNROLL:
                st_next = scores(offs[u + 1])
            m_run = jnp.maximum(m_run, jnp.max(st, axis=0, keepdims=True))
            pt = jnp.exp2(st - m_ref).astype(BF16)
            acc_sc[...] += weighted_values(offs[u], pt)
        return m_run

    m_run = lax.fori_loop(0, SEQ // (A_TK * A_UNROLL), fast_body, m_ref)
    lead_ok = jnp.max(m_run - m_ref) <= A_MAX_LEAD

    @pl.when(lead_ok)
    def _():
        finish()

    @pl.when(jnp.logical_not(lead_ok))
    def _():
        acc_sc[...] = jnp.zeros((V_ROWS, cols), F32)

        def safe_body(c, m_prev):
            off = pl.multiple_of(c * A_TK, A_TK)
            st = scores(off)
            m_new = jnp.maximum(m_prev, jnp.max(st, axis=0, keepdims=True))
            alpha = jnp.exp2(m_prev - m_new)
            pt = jnp.exp2(st - m_new).astype(BF16)
            acc_sc[...] = acc_sc[...] * alpha + weighted_values(off, pt)
            return m_new

        lax.fori_loop(0, SEQ // A_TK, safe_body, jnp.full((1, cols), -jnp.inf, F32))
        finish()


def _attn_a(qt, k, vt):
    cols = GROUP * A_TQ
    return pl.pallas_call(
        _attn_a_kernel,
        grid=(A_KV, SEQ // A_TQ),
        in_specs=[
            pl.BlockSpec((GROUP, HEAD_DIM, A_TQ), lambda j, i: (j, 0, i)),
            pl.BlockSpec((None, SEQ, HEAD_DIM), lambda j, i: (j, 0, 0)),
            pl.BlockSpec((None, V_ROWS, SEQ), lambda j, i: (j, 0, 0)),
        ],
        out_specs=pl.BlockSpec((GROUP, A_TQ, HEAD_DIM), lambda j, i: (j, i, 0)),
        out_shape=jax.ShapeDtypeStruct((A_HEADS, SEQ, HEAD_DIM), BF16),
        scratch_shapes=[
            pltpu.VMEM((HEAD_DIM, cols), BF16),
            pltpu.VMEM((V_ROWS, cols), F32),
        ],
        compiler_params=_cparams("arbitrary", "arbitrary"),
        name="attn_a",
    )(qt, k, vt)


N_QBLK = SEQ // Q_BLOCK
B_TQ = 2 * Q_BLOCK
B_KEYS = B_TQ + 2 * Q_BLOCK


def _attn_b_kernel(sink_ref, qt_ref, kl_ref, kc_ref, kr_ref, vl_ref, vc_ref, vr_ref, o_ref):
    j = pl.program_id(0)
    i = pl.program_id(1)
    k = jnp.concatenate([kl_ref[...], kc_ref[...], kr_ref[...]], axis=0)
    vt = jnp.concatenate([vl_ref[...], vc_ref[...], vr_ref[...]], axis=1)
    kj = lax.broadcasted_iota(jnp.int32, (B_KEYS, B_TQ), 0)
    qi = lax.broadcasted_iota(jnp.int32, (B_KEYS, B_TQ), 1)
    kpos = i * B_TQ - Q_BLOCK + kj
    valid = (jnp.abs(qi + Q_BLOCK - kj) <= WINDOW) & (kpos >= 0) & (kpos < SEQ)
    scores = lambda g: jnp.dot(k, qt_ref[g], preferred_element_type=F32)
    st_next = scores(0)
    for g in range(GROUP):
        sk = sink_ref[j * GROUP + g] * LOG2E
        st = jnp.where(valid, st_next, NEG_INF)
        if g + 1 < GROUP:
            st_next = scores(g + 1)
        m = jnp.maximum(jnp.max(st, axis=0, keepdims=True), sk)
        pt = jnp.exp2(st - m).astype(BF16)
        pv = jnp.dot(vt, pt, preferred_element_type=F32)
        denom = pv[HEAD_DIM:HEAD_DIM + 1] + jnp.exp2(sk - m)
        o_ref[g] = (pv[:HEAD_DIM] / denom).T.astype(BF16)


def _attn_b(sink, qt, k, vt):
    side = lambda d: (lambda j, i, s: jnp.clip(2 * i + d, 0, N_QBLK - 1))
    left, right = side(-1), side(2)
    k_side = lambda f: pl.BlockSpec((None, Q_BLOCK, HEAD_DIM), lambda j, i, s: (j, f(j, i, s), 0))
    v_side = lambda f: pl.BlockSpec((None, V_ROWS, Q_BLOCK), lambda j, i, s: (j, 0, f(j, i, s)))
    return pl.pallas_call(
        _attn_b_kernel,
        grid_spec=pltpu.PrefetchScalarGridSpec(
            num_scalar_prefetch=1,
            grid=(B_KV, SEQ // B_TQ),
            in_specs=[
                pl.BlockSpec((GROUP, HEAD_DIM, B_TQ), lambda j, i, s: (j, 0, i)),
                k_side(left),
                pl.BlockSpec((None, B_TQ, HEAD_DIM), lambda j, i, s: (j, i, 0)),
                k_side(right),
                v_side(left),
                pl.BlockSpec((None, V_ROWS, B_TQ), lambda j, i, s: (j, 0, i)),
                v_side(right),
            ],
            out_specs=pl.BlockSpec((GROUP, B_TQ, HEAD_DIM), lambda j, i, s: (j, i, 0)),
        ),
        out_shape=jax.ShapeDtypeStruct((B_HEADS, SEQ, HEAD_DIM), BF16),
        compiler_params=_cparams("arbitrary", "arbitrary"),
        name="attn_b",
    )(sink, qt, k, k, k, vt, vt, vt)


OUT_TM = 512


def _out_proj_kernel(x_ref, *refs, n_in):
    o_refs, w_ref, y_ref = refs[:n_in], refs[n_in], refs[n_in + 1]
    heads = [r[h] for r in o_refs for h in range(r.shape[0])]
    o = jnp.concatenate(heads, axis=-1)
    y_ref[...] = x_ref[...] + jnp.dot(o, w_ref[...], preferred_element_type=F32)


def _out_proj(x, outs, w):
    row = lambda i: (i, 0)
    in_specs = [pl.BlockSpec((OUT_TM, D_MODEL), row)]
    for o in outs:
        in_specs.append(pl.BlockSpec((o.shape[0], OUT_TM, HEAD_DIM), lambda i: (0, i, 0)))
    in_specs.append(pl.BlockSpec((D_MODEL, D_MODEL), lambda i: (0, 0), pipeline_mode=pl.Buffered(1)))
    return pl.pallas_call(
        functools.partial(_out_proj_kernel, n_in=len(outs)),
        grid=(SEQ // OUT_TM,),
        in_specs=in_specs,
        out_specs=pl.BlockSpec((OUT_TM, D_MODEL), row),
        out_shape=jax.ShapeDtypeStruct((SEQ, D_MODEL), F32),
        compiler_params=_cparams("arbitrary"),
        name="out_proj",
    )(x, *outs, w)


def _odd_proj_kernel(x_ref, g_ref, w_ref, q_ref, k_ref, v_ref):
    h = _rms(x_ref[...], g_ref[...]).astype(BF16)
    width = C_HEADS * HEAD_DIM
    for idx, (dst, scale) in enumerate(((q_ref, QSCALE), (k_ref, None), (v_ref, None))):
        y = jnp.dot(h, w_ref[:, idx * width:(idx + 1) * width], preferred_element_type=F32)
        if scale is not None:
            y = y * scale
        _split_heads(dst, y, 0)


def _odd_proj(x, g, w):
    hm = pl.BlockSpec((C_HEADS, PROJ_TM, HEAD_DIM), lambda i: (0, i, 0))
    hs = jax.ShapeDtypeStruct((C_HEADS, SEQ, HEAD_DIM), BF16)
    return pl.pallas_call(
        _odd_proj_kernel,
        grid=(SEQ // PROJ_TM,),
        in_specs=[
            pl.BlockSpec((PROJ_TM, D_MODEL), lambda i: (i, 0)),
            pl.BlockSpec((1, D_MODEL), lambda i: (0, 0)),
            pl.BlockSpec((D_MODEL, 3 * C_HEADS * HEAD_DIM), lambda i: (0, 0),
                         pipeline_mode=pl.Buffered(1)),
        ],
        out_specs=[hm, hm, hm],
        out_shape=[hs, hs, hs],
        compiler_params=_cparams("arbitrary"),
        name="odd_proj",
    )(x, g, w)


KH = min(NA_KH, ROWS)
N_VARIANTS = KH
C_R = 4
C_WIN = C_R + KH
C_KEYS = KH * GRID_W


def _row_start(r):
    return jnp.clip(r - KH // 2, 0, ROWS - KH)


def _win_start(i):
    return jnp.clip(i * C_R - KH // 2, 0, ROWS - C_WIN)


def _attn_c_kernel(q_ref, k_ref, v_ref, bias_ref, o_ref):
    i = pl.program_id(0)
    base = _win_start(i)

    def row_scores(t):
        r = i * C_R + t
        rs = _row_start(r)
        koff = pl.multiple_of((rs - base) * GRID_W, GRID_W)
        variant = rs - r + (KH - 1)
        s = jnp.concatenate(
            [lax.dot_general(q_ref[h, t * GRID_W:(t + 1) * GRID_W, :],
                             k_ref[h, pl.ds(koff, C_KEYS), :],
                             (((1,), (1,)), ((), ())), preferred_element_type=F32)
             for h in range(C_HEADS)], axis=0)
        return s + bias_ref[variant].reshape(C_HEADS * GRID_W, C_KEYS), koff

    nxt = row_scores(0)
    for t in range(C_R):
        s, koff = nxt
        if t + 1 < C_R:
            nxt = row_scores(t + 1)
        m = jnp.max(s, axis=1, keepdims=True)
        p = jnp.exp2(s - m)
        denom = jnp.sum(p, axis=1, keepdims=True)
        pb = p.astype(BF16)
        for h in range(C_HEADS):
            hs = slice(h * GRID_W, (h + 1) * GRID_W)
            o = jnp.dot(pb[hs], v_ref[h, pl.ds(koff, C_KEYS), :], preferred_element_type=F32)
            o_ref[h, t * GRID_W:(t + 1) * GRID_W, :] = (o / denom[hs]).astype(BF16)


def _attn_c(q, k, v, bias):
    q_spec = pl.BlockSpec((C_HEADS, C_R * GRID_W, HEAD_DIM), lambda i: (0, i, 0))
    win_spec = pl.BlockSpec(
        (pl.Element(C_HEADS), pl.Element(C_WIN * GRID_W), pl.Element(HEAD_DIM)),
        lambda i: (0, _win_start(i) * GRID_W, 0))
    bias_spec = pl.BlockSpec((N_VARIANTS, C_HEADS, GRID_W, C_KEYS), lambda i: (0, 0, 0, 0),
                             pipeline_mode=pl.Buffered(1))
    return pl.pallas_call(
        _attn_c_kernel,
        grid=(ROWS // C_R,),
        in_specs=[q_spec, win_spec, win_spec, bias_spec],
        out_specs=q_spec,
        out_shape=jax.ShapeDtypeStruct((C_HEADS, SEQ, HEAD_DIM), BF16),
        compiler_params=_cparams("arbitrary"),
        name="attn_c",
    )(q, k, v, bias)


def _neighbourhood_bias(rel_bias):
    cols = np.arange(GRID_W)
    col_start = np.clip(cols - NA_KW // 2, 0, GRID_W - NA_KW)
    kc = np.arange(GRID_W)
    in_win = (kc[None, :] >= col_start[:, None]) & (kc[None, :] < col_start[:, None] + NA_KW)
    col_idx = np.clip(kc[None, :] - cols[:, None] + (NA_KW - 1), 0, 2 * NA_KW - 2)
    t = jnp.where(in_win[None, None], rel_bias[:, :, col_idx] * LOG2E, NEG_INF)
    first = NA_KH - KH
    b = jnp.stack([t[:, first + v:first + v + KH] for v in range(N_VARIANTS)])
    b = b.transpose(0, 1, 3, 2, 4)
    return b.reshape(N_VARIANTS, C_HEADS, GRID_W, C_KEYS).astype(F32)


def _rope_tables():
    pos = jnp.arange(SEQ)

    def cos_sin(p, dim):
        inv = ROPE_THETA ** (-jnp.arange(0, dim, 2, dtype=F32) / dim)
        ang = p.astype(F32)[:, None] * inv[None, :]
        return jnp.cos(ang), jnp.sin(ang)

    c1, s1 = cos_sin(pos, HEAD_DIM)
    cr, sr = cos_sin(pos // GRID_W, HEAD_DIM // 2)
    cc, sc = cos_sin(pos % GRID_W, HEAD_DIM // 2)
    reps = LANES // HEAD_DIM
    tile = lambda parts: jnp.tile(jnp.concatenate(parts, axis=-1), (1, reps))
    return (tile([cr, cr, cc, cc]), tile([-sr, sr, -sc, sc]),
            tile([c1, c1]), tile([-s1, s1]))


def kernel(x, ffn1_norm, ffn1_w_gate, ffn1_w_up, ffn1_w_down, mix_norm, ffn2_norm, ffn2_w_gate,
           ffn2_w_up, ffn2_w_down, even_w_in, a_q_norm, a_k_norm, b_sink, even_w_out, odd_w_qkv,
           c_rel_bias, odd_w_out, final_norm):
    assert x.shape == (1, SEQ, D_MODEL)
    xs = x.reshape(SEQ, D_MODEL)
    ca, sa, cb, sb = _rope_tables()
    reps = LANES // HEAD_DIM
    group_mean = jnp.asarray(
        np.kron(np.eye(reps), np.full((HEAD_DIM, HEAD_DIM), 1.0 / HEAD_DIM)), BF16)
    fg = final_norm.reshape(1, D_MODEL)
    row = lambda v: v.reshape(1, -1)

    for layer in range(DEPTH):
        i = layer // 2
        xs = _ffn(xs, row(ffn1_norm[layer]), ffn1_w_gate[layer].astype(BF16),
                  ffn1_w_up[layer].astype(BF16), ffn1_w_down[layer].astype(BF16), fg, False)
        g = row(mix_norm[layer])
        if layer % 2 == 0:
            qa, ka, va, qb, kb, vb = _even_proj(
                xs, g, even_w_in[i].astype(BF16),
                row(jnp.tile(a_q_norm[i], reps)), row(jnp.tile(a_k_norm[i], reps)),
                group_mean, ca, sa, cb, sb)
            oa = _attn_a(qa, ka, va)
            ob = _attn_b(b_sink[i], qb, kb, vb)
            xs = _out_proj(xs, [oa, ob], even_w_out[i].astype(BF16))
        else:
            q, k, v = _odd_proj(xs, g, odd_w_qkv[i].astype(BF16))
            oc = _attn_c(q, k, v, _neighbourhood_bias(c_rel_bias[i]))
            xs = _out_proj(xs, [oc], odd_w_out[i].astype(BF16))
        xs = _ffn(xs, row(ffn2_norm[layer]), ffn2_w_gate[layer].astype(BF16),
                  ffn2_w_up[layer].astype(BF16), ffn2_w_down[layer].astype(BF16), fg,
                  layer == DEPTH - 1)
    return xs.reshape(1, SEQ, D_MODEL)
```

```python
import functools

import jax
import jax.numpy as jnp
import numpy as np
from jax import lax
from jax.experimental import pallas as pl
from jax.experimental.pallas import tpu as pltpu

D_MODEL = 1024
SEQ = 16384
DEPTH = 2
HEAD_DIM = 64
A_HEADS = 8
A_KV = 2
B_HEADS = 8
B_KV = 2
C_HEADS = 16
D_FF = 2816
GRID_W = 64
ROWS = SEQ // GRID_W
Q_BLOCK = 128
WINDOW = 128
NA_KH = 8
NA_KW = 16
ROPE_THETA = 10000.0
EPS = 1e-6
EVEN_IN = (A_HEADS + 2 * A_KV + B_HEADS + 2 * B_KV) * HEAD_DIM
ODD_IN = 3 * C_HEADS * HEAD_DIM
NEG_INF = -1e30
LOG2E = float(np.log2(np.e))
QSCALE = HEAD_DIM ** -0.5 * LOG2E

LANES = 128
MXU_N = 256
GROUP = A_HEADS // A_KV
V_ROWS = HEAD_DIM + 16

F32 = jnp.float32
BF16 = jnp.bfloat16

VMEM_LIMIT = 56 * 1024 * 1024

TM = 512
TILE_ROWS = TM // GRID_W


def _cparams(*sem):
    return pltpu.CompilerParams(dimension_semantics=sem, vmem_limit_bytes=VMEM_LIMIT)


def _rms(x, g):
    ms = jnp.mean(x * x, axis=-1, keepdims=True)
    return x * lax.rsqrt(ms + EPS) * g


def _resident(shape):
    return lambda layer: pl.BlockSpec((None,) + shape, lambda i: (layer,) + (0,) * len(shape),
                                      pipeline_mode=pl.Buffered(1))


FFN_SPLITS = (0, 6 * MXU_N, D_FF)
assert D_FF % MXU_N == 0


def _ffn_apply(x, g, wg_ref, wu_ref, wd_ref):
    h = _rms(x, g).astype(BF16)
    acc = jnp.zeros_like(x)
    for lo, hi in zip(FFN_SPLITS[:-1], FFN_SPLITS[1:]):
        sl = slice(lo, hi)
        gate = jnp.dot(h, wg_ref[:, sl], preferred_element_type=F32)
        up = jnp.dot(h, wu_ref[:, sl], preferred_element_type=F32)
        act = (gate * jax.nn.sigmoid(gate) * up).astype(BF16)
        acc = acc + jnp.dot(act, wd_ref[sl, :], preferred_element_type=F32)
    return x + 0.5 * acc


def _ffn_specs(layer):
    vec = pl.BlockSpec((None, 1, D_MODEL), lambda i: (layer, 0, 0))
    return [vec, _resident((D_MODEL, D_FF))(layer), _resident((D_MODEL, D_FF))(layer),
            _resident((D_FF, D_MODEL))(layer)]


def _split_heads(dst_ref, y, first_head):
    for t in range(y.shape[1] // HEAD_DIM):
        dst_ref[first_head + t] = y[:, t * HEAD_DIM:(t + 1) * HEAD_DIM].astype(BF16)


def _split_heads_t(dst_ref, y, first_head, ones_row=False):
    yt = y.T
    tm = y.shape[0]
    for t in range(y.shape[1] // HEAD_DIM):
        dst_ref[first_head + t, 0:HEAD_DIM, :] = yt[t * HEAD_DIM:(t + 1) * HEAD_DIM, :].astype(BF16)
        if ones_row:
            r = lax.broadcasted_iota(jnp.int32, (V_ROWS - HEAD_DIM, tm), 0)
            dst_ref[first_head + t, HEAD_DIM:V_ROWS, :] = jnp.where(r == 0, 1.0, 0.0).astype(BF16)


def _rope_factor_tables():
    lane = np.arange(LANES)
    d = lane % HEAD_DIM
    r = np.arange(ROWS, dtype=np.float64)[:, None]
    c = np.arange(GRID_W, dtype=np.float64)[:, None]
    inv_a = ROPE_THETA ** (-(2.0 * (d % 16)) / (HEAD_DIM // 2))
    sgn_a = np.where((d % 32) < 16, -1.0, 1.0)
    is_row = (d < 32)[None, :]
    a_tabs = [np.where(is_row, np.cos(r * inv_a), 0.0), np.where(is_row, sgn_a * np.sin(r * inv_a), 0.0),
              np.where(~is_row, np.cos(c * inv_a), 0.0), np.where(~is_row, sgn_a * np.sin(c * inv_a), 0.0)]
    inv_b = ROPE_THETA ** (-(2.0 * (d % 32)) / HEAD_DIM)
    sgn_b = np.where(d < 32, -1.0, 1.0)
    b_tabs = [np.cos(GRID_W * r * inv_b), sgn_b * np.sin(GRID_W * r * inv_b),
              np.cos(c * inv_b), sgn_b * np.sin(c * inv_b)]
    return [jnp.asarray(t, F32) for t in a_tabs + b_tabs]


def _rope_specs():
    row_tab = pl.BlockSpec((TILE_ROWS, LANES), lambda i: (i, 0))
    col_tab = pl.BlockSpec((GRID_W, LANES), lambda i: (0, 0))
    return [row_tab, row_tab, col_tab, col_tab] * 2


def _rope_tiles(arc, ars, acc_, acs, brc, brs, bcc, bcs):
    ca, sa, cb, sb = [], [], [], []
    for a in range(TILE_ROWS):
        row = slice(a, a + 1)
        ca.append(arc[row, :] + acc_[...])
        sa.append(ars[row, :] + acs[...])
        rc, rs = brc[row, :], brs[row, :]
        cb.append(rc * bcc[...] - rs * bcs[...])
        sb.append(rs * bcc[...] + rc * bcs[...])
    cat = lambda parts: jnp.concatenate(parts, axis=0)
    return cat(ca), cat(sa), cat(cb), cat(sb)


def _even_in_kernel(x_ref, g1_ref, wg_ref, wu_ref, wd_ref, g_ref, w_ref, gq_ref, gk_ref, gm_ref,
                    arc, ars, acc_, acs, brc, brs, bcc, bcs,
                    x1_ref, qa_ref, ka_ref, va_ref, qb_ref, kb_ref, vb_ref):
    x1 = _ffn_apply(x_ref[...], g1_ref[...], wg_ref, wu_ref, wd_ref)
    x1_ref[...] = x1
    h = _rms(x1, g_ref[...]).astype(BF16)
    proj = jnp.dot(h, w_ref[...], preferred_element_type=F32)
    lane = lax.broadcasted_iota(jnp.int32, (TM, LANES), 1)
    gm = gm_ref[...]

    def head_norm(y, gain):
        sq = y * y
        hi = sq.astype(BF16)
        lo = (sq - hi.astype(F32)).astype(BF16)
        ms = (jnp.dot(hi, gm, preferred_element_type=F32)
              + jnp.dot(lo, gm, preferred_element_type=F32))
        return y * lax.rsqrt(ms + EPS) * gain

    def rope(y, c, s, half):
        fwd = pltpu.roll(y, LANES - half, 1)
        bwd = pltpu.roll(y, half, 1)
        partner = jnp.where((lane % (2 * half)) < half, fwd, bwd)
        return y * c + partner * s

    ca, sa, cb, sb = _rope_tiles(arc, ars, acc_, acs, brc, brs, bcc, bcs)
    col = 0
    for j in range(A_HEADS // 2):
        y = rope(head_norm(proj[:, col:col + LANES], gq_ref[...]), ca, sa, HEAD_DIM // 4) * QSCALE
        _split_heads_t(qa_ref, y, 2 * j)
        col += LANES
    y = rope(head_norm(proj[:, col:col + LANES], gk_ref[...]), ca, sa, HEAD_DIM // 4)
    _split_heads(ka_ref, y, 0)
    col += LANES
    _split_heads_t(va_ref, proj[:, col:col + LANES], 0, ones_row=True)
    col += LANES
    for j in range(B_HEADS // 2):
        y = rope(proj[:, col:col + LANES], cb, sb, HEAD_DIM // 2) * QSCALE
        _split_heads_t(qb_ref, y, 2 * j)
        col += LANES
    y = rope(proj[:, col:col + LANES], cb, sb, HEAD_DIM // 2)
    _split_heads(kb_ref, y, 0)
    col += LANES
    _split_heads_t(vb_ref, proj[:, col:col + LANES], 0, ones_row=True)


def _even_in(x, layer, i, g1, wg, wu, wd, g, w, gq, gk, gm, rope_tabs):
    const = lambda idx: (0, 0)
    row = lambda idx: (idx, 0)
    k_spec = pl.BlockSpec((A_KV, TM, HEAD_DIM), lambda idx: (0, idx, 0))
    k_shape = jax.ShapeDtypeStruct((A_KV, SEQ, HEAD_DIM), BF16)
    t_spec = lambda n, rows: pl.BlockSpec((n, rows, TM), lambda idx: (0, 0, idx))
    t_shape = lambda n, rows: jax.ShapeDtypeStruct((n, rows, SEQ), BF16)
    x_spec = pl.BlockSpec((TM, D_MODEL), row)
    return pl.pallas_call(
        _even_in_kernel,
        grid=(SEQ // TM,),
        in_specs=[x_spec] + _ffn_specs(layer) + [
            pl.BlockSpec((None, 1, D_MODEL), lambda idx: (layer, 0, 0)),
            _resident((D_MODEL, EVEN_IN))(i),
            pl.BlockSpec((1, LANES), const),
            pl.BlockSpec((1, LANES), const),
            pl.BlockSpec((LANES, LANES), const),
        ] + _rope_specs(),
        out_specs=[x_spec, t_spec(A_HEADS, HEAD_DIM), k_spec, t_spec(A_KV, V_ROWS),
                   t_spec(B_HEADS, HEAD_DIM), k_spec, t_spec(B_KV, V_ROWS)],
        out_shape=[jax.ShapeDtypeStruct((SEQ, D_MODEL), F32),
                   t_shape(A_HEADS, HEAD_DIM), k_shape, t_shape(A_KV, V_ROWS),
                   t_shape(B_HEADS, HEAD_DIM), k_shape, t_shape(B_KV, V_ROWS)],
        compiler_params=_cparams("arbitrary"),
        name="even_in",
    )(x, g1, wg, wu, wd, g, w, gq, gk, gm, *rope_tabs)


A_TQ = 512
A_TK = 512
A_UNROLL = 8
A_REF_KEYS = 128
A_MAX_LEAD = 64.0


def _attn_a_kernel(qt_ref, k_ref, vt_ref, o_ref, qt_sc, acc_sc):
    cols = GROUP * A_TQ
    for g in range(GROUP):
        qt_sc[:, g * A_TQ:(g + 1) * A_TQ] = qt_ref[g]

    def scores(off, n=A_TK):
        return jnp.dot(k_ref[pl.ds(off, n), :], qt_sc[...], preferred_element_type=F32)

    def weighted_values(off, pt):
        return jnp.dot(vt_ref[:, pl.ds(off, A_TK)], pt, preferred_element_type=F32)

    def finish():
        acc = acc_sc[...]
        ot = acc[:HEAD_DIM] / acc[HEAD_DIM:HEAD_DIM + 1]
        for g in range(GROUP):
            o_ref[g] = ot[:, g * A_TQ:(g + 1) * A_TQ].T.astype(BF16)

    m_ref = jnp.max(scores(0, A_REF_KEYS), axis=0, keepdims=True)
    acc_sc[...] = jnp.zeros((V_ROWS, cols), F32)

    def fast_body(c, m_run):
        offs = [pl.multiple_of((c * A_UNROLL + u) * A_TK, A_TK) for u in range(A_UNROLL)]
        st_next = scores(offs[0])
        for u in range(A_UNROLL):
            st = st_next
            if u + 1 < A_UNROLL:
                st_next = scores(offs[u + 1])
            m_run = jnp.maximum(m_run, jnp.max(st, axis=0, keepdims=True))
            pt = jnp.exp2(st - m_ref).astype(BF16)
            acc_sc[...] += weighted_values(offs[u], pt)
        return m_run

    m_run = lax.fori_loop(0, SEQ // (A_TK * A_UNROLL), fast_body, m_ref)
    lead_ok = jnp.max(m_run - m_ref) <= A_MAX_LEAD

    @pl.when(lead_ok)
    def _():
        finish()

    @pl.when(jnp.logical_not(lead_ok))
    def _():
        acc_sc[...] = jnp.zeros((V_ROWS, cols), F32)

        def safe_body(c, m_prev):
            off = pl.multiple_of(c * A_TK, A_TK)
            st = scores(off)
            m_new = jnp.maximum(m_prev, jnp.max(st, axis=0, keepdims=True))
            alpha = jnp.exp2(m_prev - m_new)
            pt = jnp.exp2(st - m_new).astype(BF16)
            acc_sc[...] = acc_sc[...] * alpha + weighted_values(off, pt)
            return m_new

        lax.fori_loop(0, SEQ // A_TK, safe_body, jnp.full((1, cols), -jnp.inf, F32))
        finish()


def _attn_a(qt, k, vt):
    cols = GROUP * A_TQ
    return pl.pallas_call(
        _attn_a_kernel,
        grid=(A_KV, SEQ // A_TQ),
        in_specs=[
            pl.BlockSpec((GROUP, HEAD_DIM, A_TQ), lambda j, i: (j, 0, i)),
            pl.BlockSpec((None, SEQ, HEAD_DIM), lambda j, i: (j, 0, 0)),
            pl.BlockSpec((None, V_ROWS, SEQ), lambda j, i: (j, 0, 0)),
        ],
        out_specs=pl.BlockSpec((GROUP, A_TQ, HEAD_DIM), lambda j, i: (j, i, 0)),
        out_shape=jax.ShapeDtypeStruct((A_HEADS, SEQ, HEAD_DIM), BF16),
        scratch_shapes=[
            pltpu.VMEM((HEAD_DIM, cols), BF16),
            pltpu.VMEM((V_ROWS, cols), F32),
        ],
        compiler_params=_cparams("arbitrary", "arbitrary"),
        name="attn_a",
    )(qt, k, vt)


N_QBLK = SEQ // Q_BLOCK
B_TQ = 2 * Q_BLOCK
B_KEYS = B_TQ + 2 * Q_BLOCK


def _attn_b_kernel(sink_ref, qt_ref, kl_ref, kc_ref, kr_ref, vl_ref, vc_ref, vr_ref, o_ref):
    j = pl.program_id(0)
    i = pl.program_id(1)
    k = jnp.concatenate([kl_ref[...], kc_ref[...], kr_ref[...]], axis=0)
    vt = jnp.concatenate([vl_ref[...], vc_ref[...], vr_ref[...]], axis=1)
    kj = lax.broadcasted_iota(jnp.int32, (B_KEYS, B_TQ), 0)
    qi = lax.broadcasted_iota(jnp.int32, (B_KEYS, B_TQ), 1)
    kpos = i * B_TQ - Q_BLOCK + kj
    valid = (jnp.abs(qi + Q_BLOCK - kj) <= WINDOW) & (kpos >= 0) & (kpos < SEQ)
    scores = lambda g: jnp.dot(k, qt_ref[g], preferred_element_type=F32)
    st_next = scores(0)
    for g in range(GROUP):
        sk = sink_ref[j * GROUP + g] * LOG2E
        st = jnp.where(valid, st_next, NEG_INF)
        if g + 1 < GROUP:
            st_next = scores(g + 1)
        m = jnp.maximum(jnp.max(st, axis=0, keepdims=True), sk)
        pt = jnp.exp2(st - m).astype(BF16)
        pv = jnp.dot(vt, pt, preferred_element_type=F32)
        denom = pv[HEAD_DIM:HEAD_DIM + 1] + jnp.exp2(sk - m)
        o_ref[g] = (pv[:HEAD_DIM] / denom).T.astype(BF16)


def _attn_b(sink, qt, k, vt):
    side = lambda d: (lambda j, i, s: jnp.clip(2 * i + d, 0, N_QBLK - 1))
    left, right = side(-1), side(2)
    k_side = lambda f: pl.BlockSpec((None, Q_BLOCK, HEAD_DIM), lambda j, i, s: (j, f(j, i, s), 0))
    v_side = lambda f: pl.BlockSpec((None, V_ROWS, Q_BLOCK), lambda j, i, s: (j, 0, f(j, i, s)))
    return pl.pallas_call(
        _attn_b_kernel,
        grid_spec=pltpu.PrefetchScalarGridSpec(
            num_scalar_prefetch=1,
            grid=(B_KV, SEQ // B_TQ),
            in_specs=[
                pl.BlockSpec((GROUP, HEAD_DIM, B_TQ), lambda j, i, s: (j, 0, i)),
                k_side(left),
                pl.BlockSpec((None, B_TQ, HEAD_DIM), lambda j, i, s: (j, i, 0)),
                k_side(right),
                v_side(left),
                pl.BlockSpec((None, V_ROWS, B_TQ), lambda j, i, s: (j, 0, i)),
                v_side(right),
            ],
            out_specs=pl.BlockSpec((GROUP, B_TQ, HEAD_DIM), lambda j, i, s: (j, i, 0)),
        ),
        out_shape=jax.ShapeDtypeStruct((B_HEADS, SEQ, HEAD_DIM), BF16),
        compiler_params=_cparams("arbitrary", "arbitrary"),
        name="attn_b",
    )(sink, qt, k, k, k, vt, vt, vt)


def _mix_out_kernel(x_ref, *refs, n_in, final_norm):
    o_refs = refs[:n_in]
    w_ref, g2_ref, wg_ref, wu_ref, wd_ref, fg_ref, y_ref = refs[n_in:]
    heads = [r[h] for r in o_refs for h in range(r.shape[0])]
    o = jnp.concatenate(heads, axis=-1)
    x2 = x_ref[...] + jnp.dot(o, w_ref[...], preferred_element_type=F32)
    y = _ffn_apply(x2, g2_ref[...], wg_ref, wu_ref, wd_ref)
    if final_norm:
        y = _rms(y, fg_ref[...])
    y_ref[...] = y


def _mix_out(x, outs, layer, i, w, g2, wg, wu, wd, fg, final_norm):
    row = lambda idx: (idx, 0)
    x_spec = pl.BlockSpec((TM, D_MODEL), row)
    in_specs = [x_spec]
    for o in outs:
        in_specs.append(pl.BlockSpec((o.shape[0], TM, HEAD_DIM), lambda idx: (0, idx, 0)))
    in_specs.append(_resident((D_MODEL, D_MODEL))(i))
    in_specs += _ffn_specs(layer)
    in_specs.append(pl.BlockSpec((1, D_MODEL), lambda idx: (0, 0)))
    return pl.pallas_call(
        functools.partial(_mix_out_kernel, n_in=len(outs), final_norm=final_norm),
        grid=(SEQ // TM,),
        in_specs=in_specs,
        out_specs=x_spec,
        out_shape=jax.ShapeDtypeStruct((SEQ, D_MODEL), F32),
        compiler_params=_cparams("arbitrary"),
        name="mix_out",
    )(x, *outs, w, g2, wg, wu, wd, fg)


def _odd_in_kernel(x_ref, g1_ref, wg_ref, wu_ref, wd_ref, g_ref, w_ref, x1_ref, q_ref, k_ref, v_ref):
    x1 = _ffn_apply(x_ref[...], g1_ref[...], wg_ref, wu_ref, wd_ref)
    x1_ref[...] = x1
    h = _rms(x1, g_ref[...]).astype(BF16)
    width = C_HEADS * HEAD_DIM
    for idx, (dst, scale) in enumerate(((q_ref, QSCALE), (k_ref, None), (v_ref, None))):
        y = jnp.dot(h, w_ref[:, idx * width:(idx + 1) * width], preferred_element_type=F32)
        if scale is not None:
            y = y * scale
        _split_heads(dst, y, 0)


def _odd_in(x, layer, i, g1, wg, wu, wd, g, w):
    hm = pl.BlockSpec((C_HEADS, TM, HEAD_DIM), lambda idx: (0, idx, 0))
    hs = jax.ShapeDtypeStruct((C_HEADS, SEQ, HEAD_DIM), BF16)
    x_spec = pl.BlockSpec((TM, D_MODEL), lambda idx: (idx, 0))
    return pl.pallas_call(
        _odd_in_kernel,
        grid=(SEQ // TM,),
        in_specs=[x_spec] + _ffn_specs(layer) + [
            pl.BlockSpec((None, 1, D_MODEL), lambda idx: (layer, 0, 0)),
            _resident((D_MODEL, ODD_IN))(i),
        ],
        out_specs=[x_spec, hm, hm, hm],
        out_shape=[jax.ShapeDtypeStruct((SEQ, D_MODEL), F32), hs, hs, hs],
        compiler_params=_cparams("arbitrary"),
        name="odd_in",
    )(x, g1, wg, wu, wd, g, w)


KH = min(NA_KH, ROWS)
N_VARIANTS = KH
N_BIAS_ROWS = 2 * NA_KH - 1
C_R = 4
C_WIN = C_R + KH
C_KEYS = KH * GRID_W
assert KH % 2 == 0 and LANES == 2 * GRID_W


def _row_start(r):
    return jnp.clip(r - KH // 2, 0, ROWS - KH)


def _win_start(i):
    return jnp.clip(i * C_R - KH // 2, 0, ROWS - C_WIN)


def _attn_c_kernel(q_ref, k_ref, v_ref, bias_ref, o_ref):
    i = pl.program_id(0)
    base = _win_start(i)

    def row_scores(t):
        r = i * C_R + t
        rs = _row_start(r)
        koff = pl.multiple_of((rs - base) * GRID_W, GRID_W)
        first = rs - r + (NA_KH - 1)
        s = jnp.concatenate(
            [lax.dot_general(q_ref[h, t * GRID_W:(t + 1) * GRID_W, :],
                             k_ref[h, pl.ds(koff, C_KEYS), :],
                             (((1,), (1,)), ((), ())), preferred_element_type=F32)
             for h in range(C_HEADS)], axis=0)
        bias = jnp.concatenate(
            [jnp.concatenate([bias_ref[h, first + 2 * a] for a in range(KH // 2)], axis=1)
             for h in range(C_HEADS)], axis=0)
        return s + bias, koff

    nxt = row_scores(0)
    for t in range(C_R):
        s, koff = nxt
        if t + 1 < C_R:
            nxt = row_scores(t + 1)
        m = jnp.max(s, axis=1, keepdims=True)
        p = jnp.exp2(s - m)
        denom = jnp.sum(p, axis=1, keepdims=True)
        pb = p.astype(BF16)
        for h in range(C_HEADS):
            hs = slice(h * GRID_W, (h + 1) * GRID_W)
            o = jnp.dot(pb[hs], v_ref[h, pl.ds(koff, C_KEYS), :], preferred_element_type=F32)
            o_ref[h, t * GRID_W:(t + 1) * GRID_W, :] = (o / denom[hs]).astype(BF16)


def _attn_c(q, k, v, bias):
    q_spec = pl.BlockSpec((C_HEADS, C_R * GRID_W, HEAD_DIM), lambda i: (0, i, 0))
    win_spec = pl.BlockSpec(
        (pl.Element(C_HEADS), pl.Element(C_WIN * GRID_W), pl.Element(HEAD_DIM)),
        lambda i: (0, _win_start(i) * GRID_W, 0))
    bias_spec = pl.BlockSpec(bias.shape, lambda i: (0, 0, 0, 0), pipeline_mode=pl.Buffered(1))
    return pl.pallas_call(
        _attn_c_kernel,
        grid=(ROWS // C_R,),
        in_specs=[q_spec, win_spec, win_spec, bias_spec],
        out_specs=q_spec,
        out_shape=jax.ShapeDtypeStruct((C_HEADS, SEQ, HEAD_DIM), BF16),
        compiler_params=_cparams("arbitrary"),
        name="attn_c",
    )(q, k, v, bias)


def _neighbourhood_bias(rel_bias):
    cols = np.arange(GRID_W)
    col_start = np.clip(cols - NA_KW // 2, 0, GRID_W - NA_KW)
    kc = np.arange(GRID_W)
    in_win = (kc[None, :] >= col_start[:, None]) & (kc[None, :] < col_start[:, None] + NA_KW)
    col_idx = np.clip(kc[None, :] - cols[:, None] + (NA_KW - 1), 0, 2 * NA_KW - 2)
    onehot = (col_idx[None] == np.arange(2 * NA_KW - 1)[:, None, None]).astype(np.float32)
    t = jnp.einsum('hej,jck->heck', rel_bias, jnp.asarray(onehot),
                   precision=lax.Precision.HIGHEST)
    t = jnp.where(in_win[None, None], t * LOG2E, NEG_INF)
    return jnp.concatenate([t[:, :-1], t[:, 1:]], axis=-1).astype(F32)


def kernel(x, ffn1_norm, ffn1_w_gate, ffn1_w_up, ffn1_w_down, mix_norm, ffn2_norm, ffn2_w_gate,
           ffn2_w_up, ffn2_w_down, even_w_in, a_q_norm, a_k_norm, b_sink, even_w_out, odd_w_qkv,
           c_rel_bias, odd_w_out, final_norm):
    assert x.shape == (1, SEQ, D_MODEL)
    xs = x.reshape(SEQ, D_MODEL)
    rope_tabs = _rope_factor_tables()
    reps = LANES // HEAD_DIM
    group_mean = jnp.asarray(
        np.kron(np.eye(reps), np.full((HEAD_DIM, HEAD_DIM), 1.0 / HEAD_DIM)), BF16)
    fg = final_norm.reshape(1, D_MODEL)
    vec = lambda v: v.reshape(DEPTH, 1, D_MODEL)
    g1, gmix, g2 = vec(ffn1_norm), vec(mix_norm), vec(ffn2_norm)
    w1 = [w.astype(BF16) for w in (ffn1_w_gate, ffn1_w_up, ffn1_w_down)]
    w2 = [w.astype(BF16) for w in (ffn2_w_gate, ffn2_w_up, ffn2_w_down)]
    w_in, w_out_e = even_w_in.astype(BF16), even_w_out.astype(BF16)
    w_qkv, w_out_o = odd_w_qkv.astype(BF16), odd_w_out.astype(BF16)

    for layer in range(DEPTH):
        i = layer // 2
        last = layer == DEPTH - 1
        if layer % 2 == 0:
            xs, qa, ka, va, qb, kb, vb = _even_in(
                xs, layer, i, g1, *w1, gmix, w_in,
                jnp.tile(a_q_norm[i], reps).reshape(1, LANES),
                jnp.tile(a_k_norm[i], reps).reshape(1, LANES), group_mean, rope_tabs)
            outs = [_attn_a(qa, ka, va), _attn_b(b_sink[i], qb, kb, vb)]
            w_out = w_out_e
        else:
            xs, q, k, v = _odd_in(xs, layer, i, g1, *w1, gmix, w_qkv)
            outs = [_attn_c(q, k, v, _neighbourhood_bias(c_rel_bias[i]))]
            w_out = w_out_o
        xs = _mix_out(xs, outs, layer, i, w_out, g2, *w2, fg, last)
    return xs.reshape(1, SEQ, D_MODEL)
```

```python
import functools

import jax
import jax.numpy as jnp
import numpy as np
from jax import lax
from jax.experimental import pallas as pl
from jax.experimental.pallas import tpu as pltpu

D_MODEL = 1024
SEQ = 16384
DEPTH = 2
HEAD_DIM = 64
A_HEADS = 8
A_KV = 2
B_HEADS = 8
B_KV = 2
C_HEADS = 16
D_FF = 2816
GRID_W = 64
ROWS = SEQ // GRID_W
Q_BLOCK = 128
WINDOW = 128
NA_KH = 8
NA_KW = 16
ROPE_THETA = 10000.0
EPS = 1e-6
EVEN_IN = (A_HEADS + 2 * A_KV + B_HEADS + 2 * B_KV) * HEAD_DIM
C_WIDTH = C_HEADS * HEAD_DIM
ODD_IN = 3 * C_WIDTH
NEG_INF = -1e30
LOG2E = float(np.log2(np.e))
QSCALE = HEAD_DIM ** -0.5 * LOG2E

LANES = 128
MXU_N = 256
GROUP = A_HEADS // A_KV
V_ROWS = HEAD_DIM + 16
KV_LANES = A_KV * HEAD_DIM
assert KV_LANES == LANES and B_KV == A_KV

F32 = jnp.float32
BF16 = jnp.bfloat16

VMEM_LIMIT = 56 * 1024 * 1024

TM = 512
TILE_ROWS = TM // GRID_W


def _cparams(*sem):
    return pltpu.CompilerParams(dimension_semantics=sem, vmem_limit_bytes=VMEM_LIMIT)


def _rms(x, g):
    ms = jnp.mean(x * x, axis=-1, keepdims=True)
    return x * lax.rsqrt(ms + EPS) * g


def _resident(shape):
    return lambda layer: pl.BlockSpec((None,) + shape, lambda i: (layer,) + (0,) * len(shape),
                                      pipeline_mode=pl.Buffered(1))


FFN_SPLITS = (0, 6 * MXU_N, D_FF)
assert D_FF % MXU_N == 0


def _ffn_apply(x, g, wg_ref, wu_ref, wd_ref):
    h = _rms(x, g).astype(BF16)
    acc = jnp.zeros_like(x)
    for lo, hi in zip(FFN_SPLITS[:-1], FFN_SPLITS[1:]):
        sl = slice(lo, hi)
        gate = jnp.dot(h, wg_ref[:, sl], preferred_element_type=F32)
        up = jnp.dot(h, wu_ref[:, sl], preferred_element_type=F32)
        act = (gate * jax.nn.sigmoid(gate) * up).astype(BF16)
        acc = acc + jnp.dot(act, wd_ref[sl, :], preferred_element_type=F32)
    return x + 0.5 * acc


def _ffn_specs(layer):
    vec = pl.BlockSpec((None, 1, D_MODEL), lambda i: (layer, 0, 0))
    return [vec, _resident((D_MODEL, D_FF))(layer), _resident((D_MODEL, D_FF))(layer),
            _resident((D_FF, D_MODEL))(layer)]


def _split_heads_t(dst_ref, y, first_head, cols, ones_row=False):
    yt = y.T
    n = y.shape[0]
    for t in range(y.shape[1] // HEAD_DIM):
        dst_ref[first_head + t, 0:HEAD_DIM, cols] = yt[t * HEAD_DIM:(t + 1) * HEAD_DIM, :].astype(BF16)
        if ones_row:
            r = lax.broadcasted_iota(jnp.int32, (V_ROWS - HEAD_DIM, n), 0)
            dst_ref[first_head + t, HEAD_DIM:V_ROWS, cols] = jnp.where(r == 0, 1.0, 0.0).astype(BF16)


def _rope_factor_tables():
    lane = np.arange(LANES)
    d = lane % HEAD_DIM
    r = np.arange(ROWS, dtype=np.float64)[:, None]
    c = np.arange(GRID_W, dtype=np.float64)[:, None]
    inv_a = ROPE_THETA ** (-(2.0 * (d % 16)) / (HEAD_DIM // 2))
    sgn_a = np.where((d % 32) < 16, -1.0, 1.0)
    is_row = (d < 32)[None, :]
    a_tabs = [np.where(is_row, np.cos(r * inv_a), 0.0), np.where(is_row, sgn_a * np.sin(r * inv_a), 0.0),
              np.where(~is_row, np.cos(c * inv_a), 0.0), np.where(~is_row, sgn_a * np.sin(c * inv_a), 0.0)]
    inv_b = ROPE_THETA ** (-(2.0 * (d % 32)) / HEAD_DIM)
    sgn_b = np.where(d < 32, -1.0, 1.0)
    b_tabs = [np.cos(GRID_W * r * inv_b), sgn_b * np.sin(GRID_W * r * inv_b),
              np.cos(c * inv_b), sgn_b * np.sin(c * inv_b)]
    return [jnp.asarray(t, F32) for t in a_tabs + b_tabs]


def _rope_specs():
    row_tab = pl.BlockSpec((TILE_ROWS, LANES), lambda i: (i, 0))
    col_tab = pl.BlockSpec((GRID_W, LANES), lambda i: (0, 0))
    return [row_tab, row_tab, col_tab, col_tab] * 2


def _rope_tiles(grid_rows, arc, ars, acc_, acs, brc, brs, bcc, bcs):
    ca, sa, cb, sb = [], [], [], []
    for a in grid_rows:
        row = slice(a, a + 1)
        ca.append(arc[row, :] + acc_[...])
        sa.append(ars[row, :] + acs[...])
        rc, rs = brc[row, :], brs[row, :]
        cb.append(rc * bcc[...] - rs * bcs[...])
        sb.append(rs * bcc[...] + rc * bcs[...])
    cat = lambda parts: jnp.concatenate(parts, axis=0)
    return cat(ca), cat(sa), cat(cb), cat(sb)


EVEN_SPLIT = 2


def _even_in_kernel(x_ref, g1_ref, wg_ref, wu_ref, wd_ref, g_ref, w_ref, gq_ref, gk_ref, gm_ref,
                    arc, ars, acc_, acs, brc, brs, bcc, bcs,
                    x1_ref, qa_ref, ka_ref, va_ref, qb_ref, kb_ref, vb_ref):
    sub = TM // EVEN_SPLIT
    lane = lax.broadcasted_iota(jnp.int32, (sub, LANES), 1)
    gm = gm_ref[...]

    def head_norm(y, gain):
        sq = y * y
        hi = sq.astype(BF16)
        lo = (sq - hi.astype(F32)).astype(BF16)
        ms = (jnp.dot(hi, gm, preferred_element_type=F32)
              + jnp.dot(lo, gm, preferred_element_type=F32))
        return y * lax.rsqrt(ms + EPS) * gain

    def rope(y, c, s, half):
        fwd = pltpu.roll(y, LANES - half, 1)
        bwd = pltpu.roll(y, half, 1)
        partner = jnp.where((lane % (2 * half)) < half, fwd, bwd)
        return y * c + partner * s

    for part in range(EVEN_SPLIT):
        rows = slice(part * sub, (part + 1) * sub)
        x1 = _ffn_apply(x_ref[rows, :], g1_ref[...], wg_ref, wu_ref, wd_ref)
        x1_ref[rows, :] = x1
        h = _rms(x1, g_ref[...]).astype(BF16)
        proj = jnp.dot(h, w_ref[...], preferred_element_type=F32)
        grid_rows = range(part * sub // GRID_W, (part + 1) * sub // GRID_W)
        ca, sa, cb, sb = _rope_tiles(grid_rows, arc, ars, acc_, acs, brc, brs, bcc, bcs)
        col = 0
        for j in range(A_HEADS // 2):
            y = rope(head_norm(proj[:, col:col + LANES], gq_ref[...]), ca, sa, HEAD_DIM // 4)
            _split_heads_t(qa_ref, y * QSCALE, 2 * j, rows)
            col += LANES
        y = rope(head_norm(proj[:, col:col + LANES], gk_ref[...]), ca, sa, HEAD_DIM // 4)
        ka_ref[rows, :] = y.astype(BF16)
        col += LANES
        _split_heads_t(va_ref, proj[:, col:col + LANES], 0, rows, ones_row=True)
        col += LANES
        for j in range(B_HEADS // 2):
            y = rope(proj[:, col:col + LANES], cb, sb, HEAD_DIM // 2) * QSCALE
            _split_heads_t(qb_ref, y, 2 * j, rows)
            col += LANES
        y = rope(proj[:, col:col + LANES], cb, sb, HEAD_DIM // 2)
        kb_ref[rows, :] = y.astype(BF16)
        col += LANES
        _split_heads_t(vb_ref, proj[:, col:col + LANES], 0, rows, ones_row=True)


def _even_in(x, layer, i, g1, wg, wu, wd, g, w, gq, gk, gm, rope_tabs):
    const = lambda idx: (0, 0)
    row = lambda idx: (idx, 0)
    k_spec = pl.BlockSpec((TM, KV_LANES), row)
    k_shape = jax.ShapeDtypeStruct((SEQ, KV_LANES), BF16)
    t_spec = lambda n, rows: pl.BlockSpec((n, rows, TM), lambda idx: (0, 0, idx))
    t_shape = lambda n, rows: jax.ShapeDtypeStruct((n, rows, SEQ), BF16)
    x_spec = pl.BlockSpec((TM, D_MODEL), row)
    return pl.pallas_call(
        _even_in_kernel,
        grid=(SEQ // TM,),
        in_specs=[x_spec] + _ffn_specs(layer) + [
            pl.BlockSpec((None, 1, D_MODEL), lambda idx: (layer, 0, 0)),
            _resident((D_MODEL, EVEN_IN))(i),
            pl.BlockSpec((1, LANES), const),
            pl.BlockSpec((1, LANES), const),
            pl.BlockSpec((LANES, LANES), const),
        ] + _rope_specs(),
        out_specs=[x_spec, t_spec(A_HEADS, HEAD_DIM), k_spec, t_spec(A_KV, V_ROWS),
                   t_spec(B_HEADS, HEAD_DIM), k_spec, t_spec(B_KV, V_ROWS)],
        out_shape=[jax.ShapeDtypeStruct((SEQ, D_MODEL), F32),
                   t_shape(A_HEADS, HEAD_DIM), k_shape, t_shape(A_KV, V_ROWS),
                   t_shape(B_HEADS, HEAD_DIM), k_shape, t_shape(B_KV, V_ROWS)],
        compiler_params=_cparams("arbitrary"),
        name="even_in",
    )(x, g1, wg, wu, wd, g, w, gq, gk, gm, *rope_tabs)


A_TQ = 512
A_TK = 512
A_UNROLL = 8
A_REF_KEYS = 128
A_MAX_LEAD = 64.0


def _attn_a_kernel(qt_ref, k_ref, vt_ref, o_ref, qt_sc, acc_sc):
    j = pl.program_id(0)
    cols = GROUP * A_TQ
    for jj in range(A_KV):
        rows = slice(jj * HEAD_DIM, (jj + 1) * HEAD_DIM)
        for g in range(GROUP):
            qt_sc[rows, g * A_TQ:(g + 1) * A_TQ] = jnp.where(j == jj, qt_ref[g], 0).astype(BF16)

    def scores(off, n=A_TK):
        return jnp.dot(k_ref[pl.ds(off, n), :], qt_sc[...], preferred_element_type=F32)

    def weighted_values(off, pt):
        return jnp.dot(vt_ref[:, pl.ds(off, A_TK)], pt, preferred_element_type=F32)

    def finish():
        acc = acc_sc[...]
        ot = acc[:HEAD_DIM] / acc[HEAD_DIM:HEAD_DIM + 1]
        head = lambda g: ot[:, g * A_TQ:(g + 1) * A_TQ]
        for p in range(GROUP // 2):
            pair = jnp.concatenate([head(2 * p), head(2 * p + 1)], axis=0)
            o_ref[:, p * LANES:(p + 1) * LANES] = pair.T.astype(BF16)

    m_ref = jnp.max(scores(0, A_REF_KEYS), axis=0, keepdims=True)
    acc_sc[...] = jnp.zeros((V_ROWS, cols), F32)

    def fast_body(c, m_run):
        offs = [pl.multiple_of((c * A_UNROLL + u) * A_TK, A_TK) for u in range(A_UNROLL)]
        st_next = scores(offs[0])
        for u in range(A_UNROLL):
            st = st_next
            if u + 1 < A_UNROLL:
                st_next = scores(offs[u + 1])
            m_run = jnp.maximum(m_run, jnp.max(st, axis=0, keepdims=True))
            pt = jnp.exp2(st - m_ref).astype(BF16)
            acc_sc[...] += weighted_values(offs[u], pt)
        return m_run

    m_run = lax.fori_loop(0, SEQ // (A_TK * A_UNROLL), fast_body, m_ref)
    lead_ok = jnp.max(m_run - m_ref) <= A_MAX_LEAD

    @pl.when(lead_ok)
    def _():
        finish()

    @pl.when(jnp.logical_not(lead_ok))
    def _():
        acc_sc[...] = jnp.zeros((V_ROWS, cols), F32)

        def safe_body(c, m_prev):
            off = pl.multiple_of(c * A_TK, A_TK)
            st = scores(off)
            m_new = jnp.maximum(m_prev, jnp.max(st, axis=0, keepdims=True))
            alpha = jnp.exp2(m_prev - m_new)
            pt = jnp.exp2(st - m_new).astype(BF16)
            acc_sc[...] = acc_sc[...] * alpha + weighted_values(off, pt)
            return m_new

        lax.fori_loop(0, SEQ // A_TK, safe_body, jnp.full((1, cols), -jnp.inf, F32))
        finish()


def _attn_a(qt, k, vt):
    cols = GROUP * A_TQ
    return pl.pallas_call(
        _attn_a_kernel,
        grid=(A_KV, SEQ // A_TQ),
        in_specs=[
            pl.BlockSpec((GROUP, HEAD_DIM, A_TQ), lambda j, i: (j, 0, i)),
            pl.BlockSpec((SEQ, KV_LANES), lambda j, i: (0, 0)),
            pl.BlockSpec((None, V_ROWS, SEQ), lambda j, i: (j, 0, 0)),
        ],
        out_specs=pl.BlockSpec((A_TQ, GROUP * HEAD_DIM), lambda j, i: (i, j)),
        out_shape=jax.ShapeDtypeStruct((SEQ, A_HEADS * HEAD_DIM), BF16),
        scratch_shapes=[
            pltpu.VMEM((KV_LANES, cols), BF16),
            pltpu.VMEM((V_ROWS, cols), F32),
        ],
        compiler_params=_cparams("arbitrary", "arbitrary"),
        name="attn_a",
    )(qt, k, vt)


N_QBLK = SEQ // Q_BLOCK
B_TQ = 2 * Q_BLOCK
B_KEYS = B_TQ + 2 * Q_BLOCK


def _attn_b_kernel(sink_ref, qt_ref, kl_ref, kc_ref, kr_ref, vl_ref, vc_ref, vr_ref, o_ref):
    i = pl.program_id(0)
    k = jnp.concatenate([kl_ref[...], kc_ref[...], kr_ref[...]], axis=0)
    vts = [jnp.concatenate([vl_ref[j], vc_ref[j], vr_ref[j]], axis=1)
           for j in range(B_KV)]
    kj = lax.broadcasted_iota(jnp.int32, (B_KEYS, B_TQ), 0)
    qi = lax.broadcasted_iota(jnp.int32, (B_KEYS, B_TQ), 1)
    kpos = i * B_TQ - Q_BLOCK + kj
    valid = (jnp.abs(qi + Q_BLOCK - kj) <= WINDOW) & (kpos >= 0) & (kpos < SEQ)
    zeros = jnp.zeros((HEAD_DIM, B_TQ), BF16)

    def scores(h):
        parts = [qt_ref[h] if jj == h // GROUP else zeros for jj in range(B_KV)]
        return jnp.dot(k, jnp.concatenate(parts, axis=0), preferred_element_type=F32)

    outs = []
    st_next = scores(0)
    for h in range(B_HEADS):
        sk = sink_ref[h] * LOG2E
        st = jnp.where(valid, st_next, NEG_INF)
        if h + 1 < B_HEADS:
            st_next = scores(h + 1)
        m = jnp.maximum(jnp.max(st, axis=0, keepdims=True), sk)
        pt = jnp.exp2(st - m).astype(BF16)
        pv = jnp.dot(vts[h // GROUP], pt, preferred_element_type=F32)
        denom = pv[HEAD_DIM:HEAD_DIM + 1] + jnp.exp2(sk - m)
        outs.append(pv[:HEAD_DIM] / denom)
    for p in range(B_HEADS // 2):
        pair = jnp.concatenate([outs[2 * p], outs[2 * p + 1]], axis=0)
        o_ref[:, p * LANES:(p + 1) * LANES] = pair.T.astype(BF16)


def _attn_b(sink, qt, k, vt):
    side = lambda d: (lambda i, s: jnp.clip(2 * i + d, 0, N_QBLK - 1))
    left, right = side(-1), side(2)
    k_side = lambda f: pl.BlockSpec((Q_BLOCK, KV_LANES), lambda i, s: (f(i, s), 0))
    v_side = lambda f: pl.BlockSpec((B_KV, V_ROWS, Q_BLOCK), lambda i, s: (0, 0, f(i, s)))
    return pl.pallas_call(
        _attn_b_kernel,
        grid_spec=pltpu.PrefetchScalarGridSpec(
            num_scalar_prefetch=1,
            grid=(SEQ // B_TQ,),
            in_specs=[
                pl.BlockSpec((B_HEADS, HEAD_DIM, B_TQ), lambda i, s: (0, 0, i)),
                k_side(left),
                pl.BlockSpec((B_TQ, KV_LANES), lambda i, s: (i, 0)),
                k_side(right),
                v_side(left),
                pl.BlockSpec((B_KV, V_ROWS, B_TQ), lambda i, s: (0, 0, i)),
                v_side(right),
            ],
            out_specs=pl.BlockSpec((B_TQ, B_HEADS * HEAD_DIM), lambda i, s: (i, 0)),
        ),
        out_shape=jax.ShapeDtypeStruct((SEQ, B_HEADS * HEAD_DIM), BF16),
        compiler_params=_cparams("arbitrary"),
        name="attn_b",
    )(sink, qt, k, k, k, vt, vt, vt)


def _mix_out_kernel(x_ref, *refs, n_in, final_norm):
    o_refs = refs[:n_in]
    w_ref, g2_ref, wg_ref, wu_ref, wd_ref, fg_ref, y_ref = refs[n_in:]
    o = jnp.concatenate([r[...] for r in o_refs], axis=-1)
    x2 = x_ref[...] + jnp.dot(o, w_ref[...], preferred_element_type=F32)
    y = _ffn_apply(x2, g2_ref[...], wg_ref, wu_ref, wd_ref)
    if final_norm:
        y = _rms(y, fg_ref[...])
    y_ref[...] = y


def _mix_out(x, outs, layer, i, w, g2, wg, wu, wd, fg, final_norm):
    row = lambda idx: (idx, 0)
    x_spec = pl.BlockSpec((TM, D_MODEL), row)
    in_specs = [x_spec]
    for o in outs:
        in_specs.append(pl.BlockSpec((TM, o.shape[1]), row))
    in_specs.append(_resident((D_MODEL, D_MODEL))(i))
    in_specs += _ffn_specs(layer)
    in_specs.append(pl.BlockSpec((1, D_MODEL), lambda idx: (0, 0)))
    return pl.pallas_call(
        functools.partial(_mix_out_kernel, n_in=len(outs), final_norm=final_norm),
        grid=(SEQ // TM,),
        in_specs=in_specs,
        out_specs=x_spec,
        out_shape=jax.ShapeDtypeStruct((SEQ, D_MODEL), F32),
        compiler_params=_cparams("arbitrary"),
        name="mix_out",
    )(x, *outs, w, g2, wg, wu, wd, fg)


def _odd_in_kernel(x_ref, g1_ref, wg_ref, wu_ref, wd_ref, g_ref, w_ref, x1_ref, q_ref, k_ref, v_ref):
    x1 = _ffn_apply(x_ref[...], g1_ref[...], wg_ref, wu_ref, wd_ref)
    x1_ref[...] = x1
    h = _rms(x1, g_ref[...]).astype(BF16)
    for idx, (dst, scale) in enumerate(((q_ref, QSCALE), (k_ref, None), (v_ref, None))):
        y = jnp.dot(h, w_ref[:, idx * C_WIDTH:(idx + 1) * C_WIDTH], preferred_element_type=F32)
        if scale is not None:
            y = y * scale
        dst[...] = y.astype(BF16)


def _odd_in(x, layer, i, g1, wg, wu, wd, g, w):
    x_spec = pl.BlockSpec((TM, D_MODEL), lambda idx: (idx, 0))
    hm = pl.BlockSpec((TM, C_WIDTH), lambda idx: (idx, 0))
    hs = jax.ShapeDtypeStruct((SEQ, C_WIDTH), BF16)
    return pl.pallas_call(
        _odd_in_kernel,
        grid=(SEQ // TM,),
        in_specs=[x_spec] + _ffn_specs(layer) + [
            pl.BlockSpec((None, 1, D_MODEL), lambda idx: (layer, 0, 0)),
            _resident((D_MODEL, ODD_IN))(i),
        ],
        out_specs=[x_spec, hm, hm, hm],
        out_shape=[jax.ShapeDtypeStruct((SEQ, D_MODEL), F32), hs, hs, hs],
        compiler_params=_cparams("arbitrary"),
        name="odd_in",
    )(x, g1, wg, wu, wd, g, w)


KH = min(NA_KH, ROWS)
N_VARIANTS = KH
N_BIAS_ROWS = 2 * NA_KH - 1
C_R = 4
C_WIN = C_R + KH
C_KEYS = KH * GRID_W
assert KH % 2 == 0 and LANES == 2 * GRID_W


def _row_start(r):
    return jnp.clip(r - KH // 2, 0, ROWS - KH)


def _win_start(i):
    return jnp.clip(i * C_R - KH // 2, 0, ROWS - C_WIN)


def _attn_c_kernel(q_ref, k_ref, v_ref, bias_ref, o_ref):
    i = pl.program_id(0)
    base = _win_start(i)
    low = lax.broadcasted_iota(jnp.int32, (GRID_W, LANES), 1) < HEAD_DIM

    def row_scores(t):
        r = i * C_R + t
        rs = _row_start(r)
        koff = pl.multiple_of((rs - base) * GRID_W, GRID_W)
        first = rs - r + (NA_KH - 1)
        parts = []
        for pr in range(C_HEADS // 2):
            lanes = slice(pr * LANES, (pr + 1) * LANES)
            q = q_ref[t * GRID_W:(t + 1) * GRID_W, lanes]
            q2 = jnp.concatenate([jnp.where(low, q, 0), jnp.where(low, 0, q)], axis=0)
            parts.append(lax.dot_general(q2.astype(BF16), k_ref[pl.ds(koff, C_KEYS), lanes],
                                         (((1,), (1,)), ((), ())), preferred_element_type=F32))
        bias = jnp.concatenate(
            [jnp.concatenate([bias_ref[h, first + 2 * a] for a in range(KH // 2)], axis=1)
             for h in range(C_HEADS)], axis=0)
        return jnp.concatenate(parts, axis=0) + bias, koff

    nxt = row_scores(0)
    for t in range(C_R):
        s, koff = nxt
        if t + 1 < C_R:
            nxt = row_scores(t + 1)
        m = jnp.max(s, axis=1, keepdims=True)
        p = jnp.exp2(s - m)
        denom = jnp.sum(p, axis=1, keepdims=True)
        pb = p.astype(BF16)
        for pr in range(C_HEADS // 2):
            lanes = slice(pr * LANES, (pr + 1) * LANES)
            rows = slice(pr * LANES, (pr + 1) * LANES)
            o2 = jnp.dot(pb[rows], v_ref[pl.ds(koff, C_KEYS), lanes],
                         preferred_element_type=F32) / denom[rows]
            o_ref[t * GRID_W:(t + 1) * GRID_W, lanes] = jnp.where(
                low, o2[:GRID_W], o2[GRID_W:]).astype(BF16)


def _attn_c(q, k, v, bias):
    q_spec = pl.BlockSpec((C_R * GRID_W, C_WIDTH), lambda i: (i, 0))
    win_spec = pl.BlockSpec((pl.Element(C_WIN * GRID_W), pl.Element(C_WIDTH)),
                            lambda i: (_win_start(i) * GRID_W, 0))
    bias_spec = pl.BlockSpec(bias.shape, lambda i: (0, 0, 0, 0), pipeline_mode=pl.Buffered(1))
    return pl.pallas_call(
        _attn_c_kernel,
        grid=(ROWS // C_R,),
        in_specs=[q_spec, win_spec, win_spec, bias_spec],
        out_specs=q_spec,
        out_shape=jax.ShapeDtypeStruct((SEQ, C_WIDTH), BF16),
        compiler_params=_cparams("arbitrary"),
        name="attn_c",
    )(q, k, v, bias)


def _neighbourhood_bias(rel_bias):
    cols = np.arange(GRID_W)
    col_start = np.clip(cols - NA_KW // 2, 0, GRID_W - NA_KW)
    kc = np.arange(GRID_W)
    in_win = (kc[None, :] >= col_start[:, None]) & (kc[None, :] < col_start[:, None] + NA_KW)
    col_idx = np.clip(kc[None, :] - cols[:, None] + (NA_KW - 1), 0, 2 * NA_KW - 2)
    onehot = (col_idx[None] == np.arange(2 * NA_KW - 1)[:, None, None]).astype(np.float32)
    t = jnp.einsum('hej,jck->heck', rel_bias, jnp.asarray(onehot),
                   precision=lax.Precision.HIGHEST)
    t = jnp.where(in_win[None, None], t * LOG2E, NEG_INF)
    return jnp.concatenate([t[:, :-1], t[:, 1:]], axis=-1).astype(F32)


def kernel(x, ffn1_norm, ffn1_w_gate, ffn1_w_up, ffn1_w_down, mix_norm, ffn2_norm, ffn2_w_gate,
           ffn2_w_up, ffn2_w_down, even_w_in, a_q_norm, a_k_norm, b_sink, even_w_out, odd_w_qkv,
           c_rel_bias, odd_w_out, final_norm):
    assert x.shape == (1, SEQ, D_MODEL)
    xs = x.reshape(SEQ, D_MODEL)
    rope_tabs = _rope_factor_tables()
    reps = LANES // HEAD_DIM
    group_mean = jnp.asarray(
        np.kron(np.eye(reps), np.full((HEAD_DIM, HEAD_DIM), 1.0 / HEAD_DIM)), BF16)
    fg = final_norm.reshape(1, D_MODEL)
    vec = lambda v: v.reshape(DEPTH, 1, D_MODEL)
    g1, gmix, g2 = vec(ffn1_norm), vec(mix_norm), vec(ffn2_norm)
    w1 = [w.astype(BF16) for w in (ffn1_w_gate, ffn1_w_up, ffn1_w_down)]
    w2 = [w.astype(BF16) for w in (ffn2_w_gate, ffn2_w_up, ffn2_w_down)]
    w_in, w_out_e = even_w_in.astype(BF16), even_w_out.astype(BF16)
    w_qkv, w_out_o = odd_w_qkv.astype(BF16), odd_w_out.astype(BF16)

    for layer in range(DEPTH):
        i = layer // 2
        last = layer == DEPTH - 1
        if layer % 2 == 0:
            xs, qa, ka, va, qb, kb, vb = _even_in(
                xs, layer, i, g1, *w1, gmix, w_in,
                jnp.tile(a_q_norm[i], reps).reshape(1, LANES),
                jnp.tile(a_k_norm[i], reps).reshape(1, LANES), group_mean, rope_tabs)
            outs = [_attn_a(qa, ka, va), _attn_b(b_sink[i], qb, kb, vb)]
            w_out = w_out_e
        else:
            xs, q, k, v = _odd_in(xs, layer, i, g1, *w1, gmix, w_qkv)
            outs = [_attn_c(q, k, v, _neighbourhood_bias(c_rel_bias[i]))]
            w_out = w_out_o
        xs = _mix_out(xs, outs, layer, i, w_out, g2, *w2, fg, last)
    return xs.reshape(1, SEQ, D_MODEL)
```

```python
import functools

import jax
import jax.numpy as jnp
import numpy as np
from jax import lax
from jax.experimental import pallas as pl
from jax.experimental.pallas import tpu as pltpu

D_MODEL = 1024
SEQ = 16384
DEPTH = 2
HEAD_DIM = 64
A_HEADS = 8
A_KV = 2
B_HEADS = 8
B_KV = 2
C_HEADS = 16
D_FF = 2816
GRID_W = 64
ROWS = SEQ // GRID_W
Q_BLOCK = 128
WINDOW = 128
NA_KH = 8
NA_KW = 16
ROPE_THETA = 10000.0
EPS = 1e-6
EVEN_IN = (A_HEADS + 2 * A_KV + B_HEADS + 2 * B_KV) * HEAD_DIM
C_WIDTH = C_HEADS * HEAD_DIM
ODD_IN = 3 * C_WIDTH
NEG_INF = -1e30
LOG2E = float(np.log2(np.e))
QSCALE = HEAD_DIM ** -0.5 * LOG2E

LANES = 128
MXU_N = 256
GROUP = A_HEADS // A_KV
V_ROWS = HEAD_DIM + 16
KV_LANES = A_KV * HEAD_DIM
assert KV_LANES == LANES and B_KV == A_KV

F32 = jnp.float32
BF16 = jnp.bfloat16

VMEM_LIMIT = 56 * 1024 * 1024

TM = 512
TILE_ROWS = TM // GRID_W


def _cparams(*sem):
    return pltpu.CompilerParams(dimension_semantics=sem, vmem_limit_bytes=VMEM_LIMIT)


def _rms(x, g):
    ms = jnp.mean(x * x, axis=-1, keepdims=True)
    return x * lax.rsqrt(ms + EPS) * g


def _resident(shape):
    return lambda layer: pl.BlockSpec((None,) + shape, lambda i: (layer,) + (0,) * len(shape),
                                      pipeline_mode=pl.Buffered(1))


FFN_SPLITS = (0, 6 * MXU_N, D_FF)
assert D_FF % MXU_N == 0


def _ffn_apply(x, g, wg_ref, wu_ref, wd_ref):
    h = _rms(x, g).astype(BF16)
    acc = jnp.zeros_like(x)
    for lo, hi in zip(FFN_SPLITS[:-1], FFN_SPLITS[1:]):
        sl = slice(lo, hi)
        gate = jnp.dot(h, wg_ref[:, sl], preferred_element_type=F32)
        up = jnp.dot(h, wu_ref[:, sl], preferred_element_type=F32)
        act = (gate * jax.nn.sigmoid(gate) * up).astype(BF16)
        acc = acc + jnp.dot(act, wd_ref[sl, :], preferred_element_type=F32)
    return x + 0.5 * acc


def _ffn_specs(layer):
    vec = pl.BlockSpec((None, 1, D_MODEL), lambda i: (layer, 0, 0))
    return [vec, _resident((D_MODEL, D_FF))(layer), _resident((D_MODEL, D_FF))(layer),
            _resident((D_FF, D_MODEL))(layer)]


def _split_heads_t(dst_ref, y, first_head, cols, ones_row=False):
    yt = y.T
    n = y.shape[0]
    for t in range(y.shape[1] // HEAD_DIM):
        dst_ref[first_head + t, 0:HEAD_DIM, cols] = yt[t * HEAD_DIM:(t + 1) * HEAD_DIM, :].astype(BF16)
        if ones_row:
            r = lax.broadcasted_iota(jnp.int32, (V_ROWS - HEAD_DIM, n), 0)
            dst_ref[first_head + t, HEAD_DIM:V_ROWS, cols] = jnp.where(r == 0, 1.0, 0.0).astype(BF16)


def _rope_factor_tables():
    lane = np.arange(LANES)
    d = lane % HEAD_DIM
    r = np.arange(ROWS, dtype=np.float64)[:, None]
    c = np.arange(GRID_W, dtype=np.float64)[:, None]
    inv_a = ROPE_THETA ** (-(2.0 * (d % 16)) / (HEAD_DIM // 2))
    sgn_a = np.where((d % 32) < 16, -1.0, 1.0)
    is_row = (d < 32)[None, :]
    a_tabs = [np.where(is_row, np.cos(r * inv_a), 0.0), np.where(is_row, sgn_a * np.sin(r * inv_a), 0.0),
              np.where(~is_row, np.cos(c * inv_a), 0.0), np.where(~is_row, sgn_a * np.sin(c * inv_a), 0.0)]
    inv_b = ROPE_THETA ** (-(2.0 * (d % 32)) / HEAD_DIM)
    sgn_b = np.where(d < 32, -1.0, 1.0)
    b_tabs = [np.cos(GRID_W * r * inv_b), sgn_b * np.sin(GRID_W * r * inv_b),
              np.cos(c * inv_b), sgn_b * np.sin(c * inv_b)]
    return [jnp.asarray(t, F32) for t in a_tabs + b_tabs]


def _rope_specs():
    row_tab = pl.BlockSpec((TILE_ROWS, LANES), lambda i: (i, 0))
    col_tab = pl.BlockSpec((GRID_W, LANES), lambda i: (0, 0))
    return [row_tab, row_tab, col_tab, col_tab] * 2


def _rope_tiles(grid_rows, arc, ars, acc_, acs, brc, brs, bcc, bcs):
    ca, sa, cb, sb = [], [], [], []
    for a in grid_rows:
        row = slice(a, a + 1)
        ca.append(arc[row, :] + acc_[...])
        sa.append(ars[row, :] + acs[...])
        rc, rs = brc[row, :], brs[row, :]
        cb.append(rc * bcc[...] - rs * bcs[...])
        sb.append(rs * bcc[...] + rc * bcs[...])
    cat = lambda parts: jnp.concatenate(parts, axis=0)
    return cat(ca), cat(sa), cat(cb), cat(sb)


EVEN_SPLIT = 2


def _even_in_kernel(x_ref, g1_ref, wg_ref, wu_ref, wd_ref, g_ref, w_ref, gq_ref, gk_ref, gm_ref,
                    arc, ars, acc_, acs, brc, brs, bcc, bcs,
                    x1_ref, qa_ref, ka_ref, va_ref, qb_ref, kb_ref, vb_ref):
    sub = TM // EVEN_SPLIT
    lane = lax.broadcasted_iota(jnp.int32, (sub, LANES), 1)
    gm = gm_ref[...]

    def head_norm(y, gain):
        sq = y * y
        hi = sq.astype(BF16)
        lo = (sq - hi.astype(F32)).astype(BF16)
        ms = (jnp.dot(hi, gm, preferred_element_type=F32)
              + jnp.dot(lo, gm, preferred_element_type=F32))
        return y * lax.rsqrt(ms + EPS) * gain

    def rope(y, c, s, half):
        fwd = pltpu.roll(y, LANES - half, 1)
        bwd = pltpu.roll(y, half, 1)
        partner = jnp.where((lane % (2 * half)) < half, fwd, bwd)
        return y * c + partner * s

    for part in range(EVEN_SPLIT):
        rows = slice(part * sub, (part + 1) * sub)
        x1 = _ffn_apply(x_ref[rows, :], g1_ref[...], wg_ref, wu_ref, wd_ref)
        x1_ref[rows, :] = x1
        h = _rms(x1, g_ref[...]).astype(BF16)
        proj = jnp.dot(h, w_ref[...], preferred_element_type=F32)
        grid_rows = range(part * sub // GRID_W, (part + 1) * sub // GRID_W)
        ca, sa, cb, sb = _rope_tiles(grid_rows, arc, ars, acc_, acs, brc, brs, bcc, bcs)
        col = 0
        for j in range(A_HEADS // 2):
            y = rope(head_norm(proj[:, col:col + LANES], gq_ref[...]), ca, sa, HEAD_DIM // 4)
            _split_heads_t(qa_ref, y * QSCALE, 2 * j, rows)
            col += LANES
        y = rope(head_norm(proj[:, col:col + LANES], gk_ref[...]), ca, sa, HEAD_DIM // 4)
        ka_ref[rows, :] = y.astype(BF16)
        col += LANES
        _split_heads_t(va_ref, proj[:, col:col + LANES], 0, rows, ones_row=True)
        col += LANES
        for j in range(B_HEADS // 2):
            y = rope(proj[:, col:col + LANES], cb, sb, HEAD_DIM // 2) * QSCALE
            _split_heads_t(qb_ref, y, 2 * j, rows)
            col += LANES
        y = rope(proj[:, col:col + LANES], cb, sb, HEAD_DIM // 2)
        kb_ref[rows, :] = y.astype(BF16)
        col += LANES
        _split_heads_t(vb_ref, proj[:, col:col + LANES], 0, rows, ones_row=True)


def _even_in(x, layer, i, g1, wg, wu, wd, g, w, gq, gk, gm, rope_tabs):
    const = lambda idx: (0, 0)
    row = lambda idx: (idx, 0)
    k_spec = pl.BlockSpec((TM, KV_LANES), row)
    k_shape = jax.ShapeDtypeStruct((SEQ, KV_LANES), BF16)
    t_spec = lambda n, rows: pl.BlockSpec((n, rows, TM), lambda idx: (0, 0, idx))
    t_shape = lambda n, rows: jax.ShapeDtypeStruct((n, rows, SEQ), BF16)
    x_spec = pl.BlockSpec((TM, D_MODEL), row)
    return pl.pallas_call(
        _even_in_kernel,
        grid=(SEQ // TM,),
        in_specs=[x_spec] + _ffn_specs(layer) + [
            pl.BlockSpec((None, 1, D_MODEL), lambda idx: (layer, 0, 0)),
            _resident((D_MODEL, EVEN_IN))(i),
            pl.BlockSpec((1, LANES), const),
            pl.BlockSpec((1, LANES), const),
            pl.BlockSpec((LANES, LANES), const),
        ] + _rope_specs(),
        out_specs=[x_spec, t_spec(A_HEADS, HEAD_DIM), k_spec, t_spec(A_KV, V_ROWS),
                   t_spec(B_HEADS, HEAD_DIM), k_spec, t_spec(B_KV, V_ROWS)],
        out_shape=[jax.ShapeDtypeStruct((SEQ, D_MODEL), F32),
                   t_shape(A_HEADS, HEAD_DIM), k_shape, t_shape(A_KV, V_ROWS),
                   t_shape(B_HEADS, HEAD_DIM), k_shape, t_shape(B_KV, V_ROWS)],
        compiler_params=_cparams("arbitrary"),
        name="even_in",
    )(x, g1, wg, wu, wd, g, w, gq, gk, gm, *rope_tabs)


A_TQ = 512
A_TK = 1024
A_UNROLL = 4
A_REF_KEYS = 128
A_MAX_SUM = 2.0 ** 100


def _attn_a_kernel(qt_ref, k_ref, vt_ref, o_ref, qt_sc, acc_sc):
    j = pl.program_id(0)
    cols = GROUP * A_TQ
    for jj in range(A_KV):
        rows = slice(jj * HEAD_DIM, (jj + 1) * HEAD_DIM)
        for g in range(GROUP):
            qt_sc[rows, g * A_TQ:(g + 1) * A_TQ] = jnp.where(j == jj, qt_ref[g], 0).astype(BF16)

    def scores(off, n=A_TK):
        return jnp.dot(k_ref[pl.ds(off, n), :], qt_sc[...], preferred_element_type=F32)

    def weighted_values(off, pt):
        return jnp.dot(vt_ref[:, pl.ds(off, A_TK)], pt, preferred_element_type=F32)

    def finish():
        acc = acc_sc[...]
        ot = acc[:HEAD_DIM] / acc[HEAD_DIM:HEAD_DIM + 1]
        head = lambda g: ot[:, g * A_TQ:(g + 1) * A_TQ]
        for p in range(GROUP // 2):
            pair = jnp.concatenate([head(2 * p), head(2 * p + 1)], axis=0)
            o_ref[:, p * LANES:(p + 1) * LANES] = pair.T.astype(BF16)

    m_ref = jnp.max(scores(0, A_REF_KEYS), axis=0, keepdims=True)
    acc_sc[...] = jnp.zeros((V_ROWS, cols), F32)

    def fast_body(c, carry):
        for u in range(A_UNROLL):
            off = pl.multiple_of((c * A_UNROLL + u) * A_TK, A_TK)
            pt = jnp.exp2(scores(off) - m_ref).astype(BF16)
            acc_sc[...] += weighted_values(off, pt)
        return carry

    lax.fori_loop(0, SEQ // (A_TK * A_UNROLL), fast_body, 0)
    lead_ok = jnp.max(jnp.abs(acc_sc[...])) < A_MAX_SUM

    @pl.when(lead_ok)
    def _():
        finish()

    @pl.when(jnp.logical_not(lead_ok))
    def _():
        acc_sc[...] = jnp.zeros((V_ROWS, cols), F32)

        def safe_body(c, m_prev):
            off = pl.multiple_of(c * A_TK, A_TK)
            st = scores(off)
            m_new = jnp.maximum(m_prev, jnp.max(st, axis=0, keepdims=True))
            alpha = jnp.exp2(m_prev - m_new)
            pt = jnp.exp2(st - m_new).astype(BF16)
            acc_sc[...] = acc_sc[...] * alpha + weighted_values(off, pt)
            return m_new

        lax.fori_loop(0, SEQ // A_TK, safe_body, jnp.full((1, cols), -jnp.inf, F32))
        finish()


def _attn_a(qt, k, vt):
    cols = GROUP * A_TQ
    return pl.pallas_call(
        _attn_a_kernel,
        grid=(A_KV, SEQ // A_TQ),
        in_specs=[
            pl.BlockSpec((GROUP, HEAD_DIM, A_TQ), lambda j, i: (j, 0, i)),
            pl.BlockSpec((SEQ, KV_LANES), lambda j, i: (0, 0)),
            pl.BlockSpec((None, V_ROWS, SEQ), lambda j, i: (j, 0, 0)),
        ],
        out_specs=pl.BlockSpec((A_TQ, GROUP * HEAD_DIM), lambda j, i: (i, j)),
        out_shape=jax.ShapeDtypeStruct((SEQ, A_HEADS * HEAD_DIM), BF16),
        scratch_shapes=[
            pltpu.VMEM((KV_LANES, cols), BF16),
            pltpu.VMEM((V_ROWS, cols), F32),
        ],
        compiler_params=_cparams("arbitrary", "arbitrary"),
        name="attn_a",
    )(qt, k, vt)


N_QBLK = SEQ // Q_BLOCK
B_TQ = 2 * Q_BLOCK
B_KEYS = B_TQ + 2 * Q_BLOCK


def _attn_b_kernel(sink_ref, qt_ref, kl_ref, kc_ref, kr_ref, vl_ref, vc_ref, vr_ref, o_ref):
    i = pl.program_id(0)
    k = jnp.concatenate([kl_ref[...], kc_ref[...], kr_ref[...]], axis=0)
    vts = [jnp.concatenate([vl_ref[j], vc_ref[j], vr_ref[j]], axis=1)
           for j in range(B_KV)]
    kj = lax.broadcasted_iota(jnp.int32, (B_KEYS, B_TQ), 0)
    qi = lax.broadcasted_iota(jnp.int32, (B_KEYS, B_TQ), 1)
    kpos = i * B_TQ - Q_BLOCK + kj
    valid = (jnp.abs(qi + Q_BLOCK - kj) <= WINDOW) & (kpos >= 0) & (kpos < SEQ)
    zeros = jnp.zeros((HEAD_DIM, B_TQ), BF16)

    def scores(h):
        parts = [qt_ref[h] if jj == h // GROUP else zeros for jj in range(B_KV)]
        return jnp.dot(k, jnp.concatenate(parts, axis=0), preferred_element_type=F32)

    outs = []
    st_next = scores(0)
    for h in range(B_HEADS):
        sk = sink_ref[h] * LOG2E
        st = jnp.where(valid, st_next, NEG_INF)
        if h + 1 < B_HEADS:
            st_next = scores(h + 1)
        m = jnp.maximum(jnp.max(st, axis=0, keepdims=True), sk)
        pt = jnp.exp2(st - m).astype(BF16)
        pv = jnp.dot(vts[h // GROUP], pt, preferred_element_type=F32)
        denom = pv[HEAD_DIM:HEAD_DIM + 1] + jnp.exp2(sk - m)
        outs.append(pv[:HEAD_DIM] / denom)
    for p in range(B_HEADS // 2):
        pair = jnp.concatenate([outs[2 * p], outs[2 * p + 1]], axis=0)
        o_ref[:, p * LANES:(p + 1) * LANES] = pair.T.astype(BF16)


def _attn_b(sink, qt, k, vt):
    side = lambda d: (lambda i, s: jnp.clip(2 * i + d, 0, N_QBLK - 1))
    left, right = side(-1), side(2)
    k_side = lambda f: pl.BlockSpec((Q_BLOCK, KV_LANES), lambda i, s: (f(i, s), 0))
    v_side = lambda f: pl.BlockSpec((B_KV, V_ROWS, Q_BLOCK), lambda i, s: (0, 0, f(i, s)))
    return pl.pallas_call(
        _attn_b_kernel,
        grid_spec=pltpu.PrefetchScalarGridSpec(
            num_scalar_prefetch=1,
            grid=(SEQ // B_TQ,),
            in_specs=[
                pl.BlockSpec((B_HEADS, HEAD_DIM, B_TQ), lambda i, s: (0, 0, i)),
                k_side(left),
                pl.BlockSpec((B_TQ, KV_LANES), lambda i, s: (i, 0)),
                k_side(right),
                v_side(left),
                pl.BlockSpec((B_KV, V_ROWS, B_TQ), lambda i, s: (0, 0, i)),
                v_side(right),
            ],
            out_specs=pl.BlockSpec((B_TQ, B_HEADS * HEAD_DIM), lambda i, s: (i, 0)),
        ),
        out_shape=jax.ShapeDtypeStruct((SEQ, B_HEADS * HEAD_DIM), BF16),
        compiler_params=_cparams("arbitrary"),
        name="attn_b",
    )(sink, qt, k, k, k, vt, vt, vt)


def _mix_out_kernel(x_ref, *refs, n_in, final_norm):
    o_refs = refs[:n_in]
    w_ref, g2_ref, wg_ref, wu_ref, wd_ref, fg_ref, y_ref = refs[n_in:]
    o = jnp.concatenate([r[...] for r in o_refs], axis=-1)
    x2 = x_ref[...] + jnp.dot(o, w_ref[...], preferred_element_type=F32)
    y = _ffn_apply(x2, g2_ref[...], wg_ref, wu_ref, wd_ref)
    if final_norm:
        y = _rms(y, fg_ref[...])
    y_ref[...] = y


def _mix_out(x, outs, layer, i, w, g2, wg, wu, wd, fg, final_norm):
    row = lambda idx: (idx, 0)
    x_spec = pl.BlockSpec((TM, D_MODEL), row)
    in_specs = [x_spec]
    for o in outs:
        in_specs.append(pl.BlockSpec((TM, o.shape[1]), row))
    in_specs.append(_resident((D_MODEL, D_MODEL))(i))
    in_specs += _ffn_specs(layer)
    in_specs.append(pl.BlockSpec((1, D_MODEL), lambda idx: (0, 0)))
    return pl.pallas_call(
        functools.partial(_mix_out_kernel, n_in=len(outs), final_norm=final_norm),
        grid=(SEQ // TM,),
        in_specs=in_specs,
        out_specs=x_spec,
        out_shape=jax.ShapeDtypeStruct((SEQ, D_MODEL), F32),
        compiler_params=_cparams("arbitrary"),
        name="mix_out",
    )(x, *outs, w, g2, wg, wu, wd, fg)


def _odd_in_kernel(x_ref, g1_ref, wg_ref, wu_ref, wd_ref, g_ref, w_ref, x1_ref, q_ref, k_ref, v_ref):
    x1 = _ffn_apply(x_ref[...], g1_ref[...], wg_ref, wu_ref, wd_ref)
    x1_ref[...] = x1
    h = _rms(x1, g_ref[...]).astype(BF16)
    for idx, (dst, scale) in enumerate(((q_ref, QSCALE), (k_ref, None), (v_ref, None))):
        y = jnp.dot(h, w_ref[:, idx * C_WIDTH:(idx + 1) * C_WIDTH], preferred_element_type=F32)
        if scale is not None:
            y = y * scale
        dst[...] = y.astype(BF16)


def _odd_in(x, layer, i, g1, wg, wu, wd, g, w):
    x_spec = pl.BlockSpec((TM, D_MODEL), lambda idx: (idx, 0))
    hm = pl.BlockSpec((TM, C_WIDTH), lambda idx: (idx, 0))
    hs = jax.ShapeDtypeStruct((SEQ, C_WIDTH), BF16)
    return pl.pallas_call(
        _odd_in_kernel,
        grid=(SEQ // TM,),
        in_specs=[x_spec] + _ffn_specs(layer) + [
            pl.BlockSpec((None, 1, D_MODEL), lambda idx: (layer, 0, 0)),
            _resident((D_MODEL, ODD_IN))(i),
        ],
        out_specs=[x_spec, hm, hm, hm],
        out_shape=[jax.ShapeDtypeStruct((SEQ, D_MODEL), F32), hs, hs, hs],
        compiler_params=_cparams("arbitrary"),
        name="odd_in",
    )(x, g1, wg, wu, wd, g, w)


KH = min(NA_KH, ROWS)
N_VARIANTS = KH
N_BIAS_ROWS = 2 * NA_KH - 1
C_R = 4
C_WIN = C_R + KH
C_KEYS = KH * GRID_W
assert KH % 2 == 0 and LANES == 2 * GRID_W


def _row_start(r):
    return jnp.clip(r - KH // 2, 0, ROWS - KH)


def _win_start(i):
    return jnp.clip(i * C_R - KH // 2, 0, ROWS - C_WIN)


def _attn_c_kernel(q_ref, k_ref, v_ref, bias_ref, o_ref):
    i = pl.program_id(0)
    base = _win_start(i)
    low = lax.broadcasted_iota(jnp.int32, (GRID_W, LANES), 1) < HEAD_DIM

    def row_scores(t):
        r = i * C_R + t
        rs = _row_start(r)
        koff = pl.multiple_of((rs - base) * GRID_W, GRID_W)
        first = rs - r + (NA_KH - 1)
        parts = []
        for pr in range(C_HEADS // 2):
            lanes = slice(pr * LANES, (pr + 1) * LANES)
            q = q_ref[t * GRID_W:(t + 1) * GRID_W, lanes]
            q2 = jnp.concatenate([jnp.where(low, q, 0), jnp.where(low, 0, q)], axis=0)
            parts.append(lax.dot_general(q2.astype(BF16), k_ref[pl.ds(koff, C_KEYS), lanes],
                                         (((1,), (1,)), ((), ())), preferred_element_type=F32))
        bias = jnp.concatenate(
            [jnp.concatenate([bias_ref[h, first + 2 * a] for a in range(KH // 2)], axis=1)
             for h in range(C_HEADS)], axis=0)
        return jnp.concatenate(parts, axis=0) + bias, koff

    nxt = row_scores(0)
    for t in range(C_R):
        s, koff = nxt
        if t + 1 < C_R:
            nxt = row_scores(t + 1)
        m = jnp.max(s, axis=1, keepdims=True)
        p = jnp.exp2(s - m)
        denom = jnp.sum(p, axis=1, keepdims=True)
        pb = p.astype(BF16)
        for pr in range(C_HEADS // 2):
            lanes = slice(pr * LANES, (pr + 1) * LANES)
            rows = slice(pr * LANES, (pr + 1) * LANES)
            o2 = jnp.dot(pb[rows], v_ref[pl.ds(koff, C_KEYS), lanes],
                         preferred_element_type=F32) / denom[rows]
            o_ref[t * GRID_W:(t + 1) * GRID_W, lanes] = jnp.where(
                low, o2[:GRID_W], o2[GRID_W:]).astype(BF16)


def _attn_c(q, k, v, bias):
    q_spec = pl.BlockSpec((C_R * GRID_W, C_WIDTH), lambda i: (i, 0))
    win_spec = pl.BlockSpec((pl.Element(C_WIN * GRID_W), pl.Element(C_WIDTH)),
                            lambda i: (_win_start(i) * GRID_W, 0))
    bias_spec = pl.BlockSpec(bias.shape, lambda i: (0, 0, 0, 0), pipeline_mode=pl.Buffered(1))
    return pl.pallas_call(
        _attn_c_kernel,
        grid=(ROWS // C_R,),
        in_specs=[q_spec, win_spec, win_spec, bias_spec],
        out_specs=q_spec,
        out_shape=jax.ShapeDtypeStruct((SEQ, C_WIDTH), BF16),
        compiler_params=_cparams("arbitrary"),
        name="attn_c",
    )(q, k, v, bias)


def _neighbourhood_bias(rel_bias):
    cols = np.arange(GRID_W)
    col_start = np.clip(cols - NA_KW // 2, 0, GRID_W - NA_KW)
    kc = np.arange(GRID_W)
    in_win = (kc[None, :] >= col_start[:, None]) & (kc[None, :] < col_start[:, None] + NA_KW)
    col_idx = np.clip(kc[None, :] - cols[:, None] + (NA_KW - 1), 0, 2 * NA_KW - 2)
    onehot = (col_idx[None] == np.arange(2 * NA_KW - 1)[:, None, None]).astype(np.float32)
    t = jnp.einsum('hej,jck->heck', rel_bias, jnp.asarray(onehot),
                   precision=lax.Precision.HIGHEST)
    t = jnp.where(in_win[None, None], t * LOG2E, NEG_INF)
    return jnp.concatenate([t[:, :-1], t[:, 1:]], axis=-1).astype(F32)


def kernel(x, ffn1_norm, ffn1_w_gate, ffn1_w_up, ffn1_w_down, mix_norm, ffn2_norm, ffn2_w_gate,
           ffn2_w_up, ffn2_w_down, even_w_in, a_q_norm, a_k_norm, b_sink, even_w_out, odd_w_qkv,
           c_rel_bias, odd_w_out, final_norm):
    assert x.shape == (1, SEQ, D_MODEL)
    xs = x.reshape(SEQ, D_MODEL)
    rope_tabs = _rope_factor_tables()
    reps = LANES // HEAD_DIM
    group_mean = jnp.asarray(
        np.kron(np.eye(reps), np.full((HEAD_DIM, HEAD_DIM), 1.0 / HEAD_DIM)), BF16)
    fg = final_norm.reshape(1, D_MODEL)
    vec = lambda v: v.reshape(DEPTH, 1, D_MODEL)
    g1, gmix, g2 = vec(ffn1_norm), vec(mix_norm), vec(ffn2_norm)
    w1 = [w.astype(BF16) for w in (ffn1_w_gate, ffn1_w_up, ffn1_w_down)]
    w2 = [w.astype(BF16) for w in (ffn2_w_gate, ffn2_w_up, ffn2_w_down)]
    w_in, w_out_e = even_w_in.astype(BF16), even_w_out.astype(BF16)
    w_qkv, w_out_o = odd_w_qkv.astype(BF16), odd_w_out.astype(BF16)

    for layer in range(DEPTH):
        i = layer // 2
        last = layer == DEPTH - 1
        if layer % 2 == 0:
            xs, qa, ka, va, qb, kb, vb = _even_in(
                xs, layer, i, g1, *w1, gmix, w_in,
                jnp.tile(a_q_norm[i], reps).reshape(1, LANES),
                jnp.tile(a_k_norm[i], reps).reshape(1, LANES), group_mean, rope_tabs)
            outs = [_attn_a(qa, ka, va), _attn_b(b_sink[i], qb, kb, vb)]
            w_out = w_out_e
        else:
            xs, q, k, v = _odd_in(xs, layer, i, g1, *w1, gmix, w_qkv)
            outs = [_attn_c(q, k, v, _neighbourhood_bias(c_rel_bias[i]))]
            w_out = w_out_o
        xs = _mix_out(xs, outs, layer, i, w_out, g2, *w2, fg, last)
    return xs.reshape(1, SEQ, D_MODEL)
```

```python
import functools

import jax
import jax.numpy as jnp
import numpy as np
from jax import lax
from jax.experimental import pallas as pl
from jax.experimental.pallas import tpu as pltpu

D_MODEL = 1024
SEQ = 16384
DEPTH = 2
HEAD_DIM = 64
A_HEADS = 8
A_KV = 2
B_HEADS = 8
B_KV = 2
C_HEADS = 16
D_FF = 2816
GRID_W = 64
ROWS = SEQ // GRID_W
Q_BLOCK = 128
WINDOW = 128
NA_KH = 8
NA_KW = 16
ROPE_THETA = 10000.0
EPS = 1e-6
EVEN_IN = (A_HEADS + 2 * A_KV + B_HEADS + 2 * B_KV) * HEAD_DIM
C_WIDTH = C_HEADS * HEAD_DIM
ODD_IN = 3 * C_WIDTH
NEG_INF = -1e30
LOG2E = float(np.log2(np.e))
QSCALE = HEAD_DIM ** -0.5 * LOG2E

LANES = 128
MXU_N = 256
GROUP = A_HEADS // A_KV
V_ROWS = HEAD_DIM + 16
KV_LANES = A_KV * HEAD_DIM
assert KV_LANES == LANES and B_KV == A_KV

F32 = jnp.float32
BF16 = jnp.bfloat16

VMEM_LIMIT = 56 * 1024 * 1024

TM = 512
TILE_ROWS = TM // GRID_W


def _cparams(*sem):
    return pltpu.CompilerParams(dimension_semantics=sem, vmem_limit_bytes=VMEM_LIMIT)


def _rms(x, g):
    ms = jnp.mean(x * x, axis=-1, keepdims=True)
    return x * lax.rsqrt(ms + EPS) * g


def _resident(shape):
    return lambda layer: pl.BlockSpec((None,) + shape, lambda i: (layer,) + (0,) * len(shape),
                                      pipeline_mode=pl.Buffered(1))


FFN_SPLITS = (0, 6 * MXU_N, D_FF)
assert D_FF % MXU_N == 0


def _ffn_apply(x, g, wg_ref, wu_ref, wd_ref):
    h = _rms(x, g).astype(BF16)
    acc = jnp.zeros_like(x)
    for lo, hi in zip(FFN_SPLITS[:-1], FFN_SPLITS[1:]):
        sl = slice(lo, hi)
        gate = jnp.dot(h, wg_ref[:, sl], preferred_element_type=F32)
        up = jnp.dot(h, wu_ref[:, sl], preferred_element_type=F32)
        act = (gate * jax.nn.sigmoid(gate) * up).astype(BF16)
        acc = acc + jnp.dot(act, wd_ref[sl, :], preferred_element_type=F32)
    return x + 0.5 * acc


N_CAST = 16
CAST_ROWS_IN = D_MODEL // N_CAST
CAST_ROWS_DOWN = D_FF // N_CAST
assert CAST_ROWS_IN % 16 == 0 and CAST_ROWS_DOWN % 16 == 0
N_TILES = SEQ // TM
FFN_GRID = (N_CAST + N_TILES,)
FFN_SCRATCH = [pltpu.VMEM((D_MODEL, D_FF), BF16), pltpu.VMEM((D_MODEL, D_FF), BF16),
               pltpu.VMEM((D_FF, D_MODEL), BF16)]


def _tile(s):
    return jnp.maximum(s - N_CAST, 0)


def _ffn_specs(layer):
    vec = pl.BlockSpec((None, 1, D_MODEL), lambda s: (layer, 0, 0))
    chunk = lambda s: jnp.minimum(s, N_CAST - 1)
    w_in = pl.BlockSpec((None, CAST_ROWS_IN, D_FF), lambda s: (layer, chunk(s), 0))
    w_down = pl.BlockSpec((None, CAST_ROWS_DOWN, D_MODEL), lambda s: (layer, chunk(s), 0))
    return [vec, w_in, w_in, w_down]


def _cast_ffn_chunk(s, wg_ref, wu_ref, wd_ref, wg_sc, wu_sc, wd_sc):
    r_in = pl.multiple_of(s * CAST_ROWS_IN, CAST_ROWS_IN)
    r_down = pl.multiple_of(s * CAST_ROWS_DOWN, CAST_ROWS_DOWN)
    wg_sc[pl.ds(r_in, CAST_ROWS_IN), :] = wg_ref[...].astype(BF16)
    wu_sc[pl.ds(r_in, CAST_ROWS_IN), :] = wu_ref[...].astype(BF16)
    wd_sc[pl.ds(r_down, CAST_ROWS_DOWN), :] = wd_ref[...].astype(BF16)


def _split_heads_t(dst_ref, y, first_head, cols, ones_row=False):
    yt = y.T
    n = y.shape[0]
    for t in range(y.shape[1] // HEAD_DIM):
        dst_ref[first_head + t, 0:HEAD_DIM, cols] = yt[t * HEAD_DIM:(t + 1) * HEAD_DIM, :].astype(BF16)
        if ones_row:
            r = lax.broadcasted_iota(jnp.int32, (V_ROWS - HEAD_DIM, n), 0)
            dst_ref[first_head + t, HEAD_DIM:V_ROWS, cols] = jnp.where(r == 0, 1.0, 0.0).astype(BF16)


def _rope_factor_tables():
    lane = np.arange(LANES)
    d = lane % HEAD_DIM
    r = np.arange(ROWS, dtype=np.float64)[:, None]
    c = np.arange(GRID_W, dtype=np.float64)[:, None]
    inv_a = ROPE_THETA ** (-(2.0 * (d % 16)) / (HEAD_DIM // 2))
    sgn_a = np.where((d % 32) < 16, -1.0, 1.0)
    is_row = (d < 32)[None, :]
    a_tabs = [np.where(is_row, np.cos(r * inv_a), 0.0), np.where(is_row, sgn_a * np.sin(r * inv_a), 0.0),
              np.where(~is_row, np.cos(c * inv_a), 0.0), np.where(~is_row, sgn_a * np.sin(c * inv_a), 0.0)]
    inv_b = ROPE_THETA ** (-(2.0 * (d % 32)) / HEAD_DIM)
    sgn_b = np.where(d < 32, -1.0, 1.0)
    b_tabs = [np.cos(GRID_W * r * inv_b), sgn_b * np.sin(GRID_W * r * inv_b),
              np.cos(c * inv_b), sgn_b * np.sin(c * inv_b)]
    return [jnp.asarray(t, F32) for t in a_tabs + b_tabs]


def _rope_specs():
    row_tab = pl.BlockSpec((TILE_ROWS, LANES), lambda s: (_tile(s), 0))
    col_tab = pl.BlockSpec((GRID_W, LANES), lambda s: (0, 0))
    return [row_tab, row_tab, col_tab, col_tab] * 2


def _rope_tiles(grid_rows, arc, ars, acc_, acs, brc, brs, bcc, bcs):
    ca, sa, cb, sb = [], [], [], []
    for a in grid_rows:
        row = slice(a, a + 1)
        ca.append(arc[row, :] + acc_[...])
        sa.append(ars[row, :] + acs[...])
        rc, rs = brc[row, :], brs[row, :]
        cb.append(rc * bcc[...] - rs * bcs[...])
        sb.append(rs * bcc[...] + rc * bcs[...])
    cat = lambda parts: jnp.concatenate(parts, axis=0)
    return cat(ca), cat(sa), cat(cb), cat(sb)


EVEN_SPLIT = 2


def _even_in_kernel(x_ref, g1_ref, wg_ref, wu_ref, wd_ref, *refs):
    ffn_w = refs[-len(FFN_SCRATCH):]
    s = pl.program_id(0)

    @pl.when(s < N_CAST)
    def _():
        _cast_ffn_chunk(s, wg_ref, wu_ref, wd_ref, *ffn_w)

    @pl.when(s >= N_CAST)
    def _():
        _even_in_tile(x_ref, g1_ref, *ffn_w, *refs[:-len(FFN_SCRATCH)])


def _even_in_tile(x_ref, g1_ref, wg_ref, wu_ref, wd_ref, g_ref, w_ref, gq_ref, gk_ref, gm_ref,
                  arc, ars, acc_, acs, brc, brs, bcc, bcs,
                  x1_ref, qa_ref, ka_ref, va_ref, qb_ref, kb_ref, vb_ref):
    sub = TM // EVEN_SPLIT
    lane = lax.broadcasted_iota(jnp.int32, (sub, LANES), 1)
    gm = gm_ref[...]

    def head_norm(y, gain):
        sq = y * y
        hi = sq.astype(BF16)
        lo = (sq - hi.astype(F32)).astype(BF16)
        ms = (jnp.dot(hi, gm, preferred_element_type=F32)
              + jnp.dot(lo, gm, preferred_element_type=F32))
        return y * lax.rsqrt(ms + EPS) * gain

    def rope(y, c, s, half):
        fwd = pltpu.roll(y, LANES - half, 1)
        bwd = pltpu.roll(y, half, 1)
        partner = jnp.where((lane % (2 * half)) < half, fwd, bwd)
        return y * c + partner * s

    for part in range(EVEN_SPLIT):
        rows = slice(part * sub, (part + 1) * sub)
        x1 = _ffn_apply(x_ref[rows, :], g1_ref[...], wg_ref, wu_ref, wd_ref)
        x1_ref[rows, :] = x1
        h = _rms(x1, g_ref[...]).astype(BF16)
        proj = jnp.dot(h, w_ref[...], preferred_element_type=F32)
        grid_rows = range(part * sub // GRID_W, (part + 1) * sub // GRID_W)
        ca, sa, cb, sb = _rope_tiles(grid_rows, arc, ars, acc_, acs, brc, brs, bcc, bcs)
        col = 0
        for j in range(A_HEADS // 2):
            y = rope(head_norm(proj[:, col:col + LANES], gq_ref[...]), ca, sa, HEAD_DIM // 4)
            _split_heads_t(qa_ref, y * QSCALE, 2 * j, rows)
            col += LANES
        y = rope(head_norm(proj[:, col:col + LANES], gk_ref[...]), ca, sa, HEAD_DIM // 4)
        ka_ref[rows, :] = y.astype(BF16)
        col += LANES
        _split_heads_t(va_ref, proj[:, col:col + LANES], 0, rows, ones_row=True)
        col += LANES
        for j in range(B_HEADS // 2):
            y = rope(proj[:, col:col + LANES], cb, sb, HEAD_DIM // 2) * QSCALE
            _split_heads_t(qb_ref, y, 2 * j, rows)
            col += LANES
        y = rope(proj[:, col:col + LANES], cb, sb, HEAD_DIM // 2)
        kb_ref[rows, :] = y.astype(BF16)
        col += LANES
        _split_heads_t(vb_ref, proj[:, col:col + LANES], 0, rows, ones_row=True)


def _even_in(x, layer, i, g1, wg, wu, wd, g, w, gq, gk, gm, rope_tabs):
    const = lambda s: (0, 0)
    row = lambda s: (_tile(s), 0)
    k_spec = pl.BlockSpec((TM, KV_LANES), row)
    k_shape = jax.ShapeDtypeStruct((SEQ, KV_LANES), BF16)
    t_spec = lambda n, rows: pl.BlockSpec((n, rows, TM), lambda s: (0, 0, _tile(s)))
    t_shape = lambda n, rows: jax.ShapeDtypeStruct((n, rows, SEQ), BF16)
    x_spec = pl.BlockSpec((TM, D_MODEL), row)
    return pl.pallas_call(
        _even_in_kernel,
        grid=FFN_GRID,
        scratch_shapes=FFN_SCRATCH,
        in_specs=[x_spec] + _ffn_specs(layer) + [
            pl.BlockSpec((None, 1, D_MODEL), lambda s: (layer, 0, 0)),
            _resident((D_MODEL, EVEN_IN))(i),
            pl.BlockSpec((1, LANES), const),
            pl.BlockSpec((1, LANES), const),
            pl.BlockSpec((LANES, LANES), const),
        ] + _rope_specs(),
        out_specs=[x_spec, t_spec(A_HEADS, HEAD_DIM), k_spec, t_spec(A_KV, V_ROWS),
                   t_spec(B_HEADS, HEAD_DIM), k_spec, t_spec(B_KV, V_ROWS)],
        out_shape=[jax.ShapeDtypeStruct((SEQ, D_MODEL), F32),
                   t_shape(A_HEADS, HEAD_DIM), k_shape, t_shape(A_KV, V_ROWS),
                   t_shape(B_HEADS, HEAD_DIM), k_shape, t_shape(B_KV, V_ROWS)],
        compiler_params=_cparams("arbitrary"),
        name="even_in",
    )(x, g1, wg, wu, wd, g, w, gq, gk, gm, *rope_tabs)


A_TQ = 512
A_TK = 1024
A_UNROLL = 4
A_REF_KEYS = 128
A_MAX_SUM = 2.0 ** 100


def _attn_a_kernel(qt_ref, k_ref, vt_ref, o_ref, qt_sc, acc_sc):
    j = pl.program_id(0)
    cols = GROUP * A_TQ
    for jj in range(A_KV):
        rows = slice(jj * HEAD_DIM, (jj + 1) * HEAD_DIM)
        for g in range(GROUP):
            qt_sc[rows, g * A_TQ:(g + 1) * A_TQ] = jnp.where(j == jj, qt_ref[g], 0).astype(BF16)

    def scores(off, n=A_TK):
        return jnp.dot(k_ref[pl.ds(off, n), :], qt_sc[...], preferred_element_type=F32)

    def weighted_values(off, pt):
        return jnp.dot(vt_ref[:, pl.ds(off, A_TK)], pt, preferred_element_type=F32)

    def finish():
        acc = acc_sc[...]
        ot = acc[:HEAD_DIM] / acc[HEAD_DIM:HEAD_DIM + 1]
        head = lambda g: ot[:, g * A_TQ:(g + 1) * A_TQ]
        for p in range(GROUP // 2):
            pair = jnp.concatenate([head(2 * p), head(2 * p + 1)], axis=0)
            o_ref[:, p * LANES:(p + 1) * LANES] = pair.T.astype(BF16)

    m_ref = jnp.max(scores(0, A_REF_KEYS), axis=0, keepdims=True)
    acc_sc[...] = jnp.zeros((V_ROWS, cols), F32)

    def fast_body(c, carry):
        for u in range(A_UNROLL):
            off = pl.multiple_of((c * A_UNROLL + u) * A_TK, A_TK)
            pt = jnp.exp2(scores(off) - m_ref).astype(BF16)
            acc_sc[...] += weighted_values(off, pt)
        return carry

    lax.fori_loop(0, SEQ // (A_TK * A_UNROLL), fast_body, 0)
    lead_ok = jnp.max(jnp.abs(acc_sc[...])) < A_MAX_SUM

    @pl.when(lead_ok)
    def _():
        finish()

    @pl.when(jnp.logical_not(lead_ok))
    def _():
        acc_sc[...] = jnp.zeros((V_ROWS, cols), F32)

        def safe_body(c, m_prev):
            off = pl.multiple_of(c * A_TK, A_TK)
            st = scores(off)
            m_new = jnp.maximum(m_prev, jnp.max(st, axis=0, keepdims=True))
            alpha = jnp.exp2(m_prev - m_new)
            pt = jnp.exp2(st - m_new).astype(BF16)
            acc_sc[...] = acc_sc[...] * alpha + weighted_values(off, pt)
            return m_new

        lax.fori_loop(0, SEQ // A_TK, safe_body, jnp.full((1, cols), -jnp.inf, F32))
        finish()


def _attn_a(qt, k, vt):
    cols = GROUP * A_TQ
    return pl.pallas_call(
        _attn_a_kernel,
        grid=(A_KV, SEQ // A_TQ),
        in_specs=[
            pl.BlockSpec((GROUP, HEAD_DIM, A_TQ), lambda j, i: (j, 0, i)),
            pl.BlockSpec((SEQ, KV_LANES), lambda j, i: (0, 0)),
            pl.BlockSpec((None, V_ROWS, SEQ), lambda j, i: (j, 0, 0)),
        ],
        out_specs=pl.BlockSpec((A_TQ, GROUP * HEAD_DIM), lambda j, i: (i, j)),
        out_shape=jax.ShapeDtypeStruct((SEQ, A_HEADS * HEAD_DIM), BF16),
        scratch_shapes=[
            pltpu.VMEM((KV_LANES, cols), BF16),
            pltpu.VMEM((V_ROWS, cols), F32),
        ],
        compiler_params=_cparams("arbitrary", "arbitrary"),
        name="attn_a",
    )(qt, k, vt)


N_QBLK = SEQ // Q_BLOCK
B_TQ = 2 * Q_BLOCK
B_KEYS = B_TQ + 2 * Q_BLOCK


def _attn_b_kernel(sink_ref, qt_ref, kl_ref, kc_ref, kr_ref, vl_ref, vc_ref, vr_ref, o_ref):
    i = pl.program_id(0)
    k = jnp.concatenate([kl_ref[...], kc_ref[...], kr_ref[...]], axis=0)
    vts = [jnp.concatenate([vl_ref[j], vc_ref[j], vr_ref[j]], axis=1)
           for j in range(B_KV)]
    kj = lax.broadcasted_iota(jnp.int32, (B_KEYS, B_TQ), 0)
    qi = lax.broadcasted_iota(jnp.int32, (B_KEYS, B_TQ), 1)
    kpos = i * B_TQ - Q_BLOCK + kj
    valid = (jnp.abs(qi + Q_BLOCK - kj) <= WINDOW) & (kpos >= 0) & (kpos < SEQ)
    zeros = jnp.zeros((HEAD_DIM, B_TQ), BF16)

    def scores(h):
        parts = [qt_ref[h] if jj == h // GROUP else zeros for jj in range(B_KV)]
        return jnp.dot(k, jnp.concatenate(parts, axis=0), preferred_element_type=F32)

    def attend(shift_by_max):
        res = []
        st_next = scores(0)
        for h in range(B_HEADS):
            sk = sink_ref[h] * LOG2E
            st = jnp.where(valid, st_next, NEG_INF)
            if h + 1 < B_HEADS:
                st_next = scores(h + 1)
            m = jnp.maximum(jnp.max(st, axis=0, keepdims=True), sk) if shift_by_max else sk
            pt = jnp.exp2(st - m).astype(BF16)
            pv = jnp.dot(vts[h // GROUP], pt, preferred_element_type=F32)
            res.append((pv, jnp.exp2(sk - m)))
        return res

    def write(res):
        outs = [pv[:HEAD_DIM] / (pv[HEAD_DIM:HEAD_DIM + 1] + sink) for pv, sink in res]
        for p in range(B_HEADS // 2):
            pair = jnp.concatenate([outs[2 * p], outs[2 * p + 1]], axis=0)
            o_ref[:, p * LANES:(p + 1) * LANES] = pair.T.astype(BF16)

    fast = attend(shift_by_max=False)
    bound = functools.reduce(jnp.maximum, [jnp.max(jnp.abs(pv)) for pv, _ in fast])
    ok = bound < A_MAX_SUM

    @pl.when(ok)
    def _():
        write(fast)

    @pl.when(jnp.logical_not(ok))
    def _():
        write(attend(shift_by_max=True))


def _attn_b(sink, qt, k, vt):
    side = lambda d: (lambda i, s: jnp.clip(2 * i + d, 0, N_QBLK - 1))
    left, right = side(-1), side(2)
    k_side = lambda f: pl.BlockSpec((Q_BLOCK, KV_LANES), lambda i, s: (f(i, s), 0))
    v_side = lambda f: pl.BlockSpec((B_KV, V_ROWS, Q_BLOCK), lambda i, s: (0, 0, f(i, s)))
    return pl.pallas_call(
        _attn_b_kernel,
        grid_spec=pltpu.PrefetchScalarGridSpec(
            num_scalar_prefetch=1,
            grid=(SEQ // B_TQ,),
            in_specs=[
                pl.BlockSpec((B_HEADS, HEAD_DIM, B_TQ), lambda i, s: (0, 0, i)),
                k_side(left),
                pl.BlockSpec((B_TQ, KV_LANES), lambda i, s: (i, 0)),
                k_side(right),
                v_side(left),
                pl.BlockSpec((B_KV, V_ROWS, B_TQ), lambda i, s: (0, 0, i)),
                v_side(right),
            ],
            out_specs=pl.BlockSpec((B_TQ, B_HEADS * HEAD_DIM), lambda i, s: (i, 0)),
        ),
        out_shape=jax.ShapeDtypeStruct((SEQ, B_HEADS * HEAD_DIM), BF16),
        compiler_params=_cparams("arbitrary"),
        name="attn_b",
    )(sink, qt, k, k, k, vt, vt, vt)


def _mix_out_kernel(x_ref, *refs, n_in, final_norm):
    o_refs = refs[:n_in]
    w_ref, g2_ref, wg_ref, wu_ref, wd_ref, fg_ref, y_ref = refs[n_in:-len(FFN_SCRATCH)]
    ffn_w = refs[-len(FFN_SCRATCH):]
    s = pl.program_id(0)

    @pl.when(s < N_CAST)
    def _():
        _cast_ffn_chunk(s, wg_ref, wu_ref, wd_ref, *ffn_w)

    @pl.when(s >= N_CAST)
    def _():
        o = jnp.concatenate([r[...] for r in o_refs], axis=-1)
        x2 = x_ref[...] + jnp.dot(o, w_ref[...], preferred_element_type=F32)
        y = _ffn_apply(x2, g2_ref[...], *ffn_w)
        if final_norm:
            y = _rms(y, fg_ref[...])
        y_ref[...] = y


def _mix_out(x, outs, layer, i, w, g2, wg, wu, wd, fg, final_norm):
    row = lambda s: (_tile(s), 0)
    x_spec = pl.BlockSpec((TM, D_MODEL), row)
    in_specs = [x_spec]
    for o in outs:
        in_specs.append(pl.BlockSpec((TM, o.shape[1]), row))
    in_specs.append(_resident((D_MODEL, D_MODEL))(i))
    in_specs += _ffn_specs(layer)
    in_specs.append(pl.BlockSpec((1, D_MODEL), lambda s: (0, 0)))
    return pl.pallas_call(
        functools.partial(_mix_out_kernel, n_in=len(outs), final_norm=final_norm),
        grid=FFN_GRID,
        scratch_shapes=FFN_SCRATCH,
        in_specs=in_specs,
        out_specs=x_spec,
        out_shape=jax.ShapeDtypeStruct((SEQ, D_MODEL), F32),
        compiler_params=_cparams("arbitrary"),
        name="mix_out",
    )(x, *outs, w, g2, wg, wu, wd, fg)


def _odd_in_kernel(x_ref, g1_ref, wg_ref, wu_ref, wd_ref, g_ref, w_ref, x1_ref, q_ref, k_ref, v_ref,
                   *ffn_w):
    s = pl.program_id(0)

    @pl.when(s < N_CAST)
    def _():
        _cast_ffn_chunk(s, wg_ref, wu_ref, wd_ref, *ffn_w)

    @pl.when(s >= N_CAST)
    def _():
        x1 = _ffn_apply(x_ref[...], g1_ref[...], *ffn_w)
        x1_ref[...] = x1
        h = _rms(x1, g_ref[...]).astype(BF16)
        for idx, (dst, scale) in enumerate(((q_ref, QSCALE), (k_ref, None), (v_ref, None))):
            y = jnp.dot(h, w_ref[:, idx * C_WIDTH:(idx + 1) * C_WIDTH],
                        preferred_element_type=F32)
            if scale is not None:
                y = y * scale
            dst[...] = y.astype(BF16)


def _odd_in(x, layer, i, g1, wg, wu, wd, g, w):
    x_spec = pl.BlockSpec((TM, D_MODEL), lambda s: (_tile(s), 0))
    hm = pl.BlockSpec((TM, C_WIDTH), lambda s: (_tile(s), 0))
    hs = jax.ShapeDtypeStruct((SEQ, C_WIDTH), BF16)
    return pl.pallas_call(
        _odd_in_kernel,
        grid=FFN_GRID,
        scratch_shapes=FFN_SCRATCH,
        in_specs=[x_spec] + _ffn_specs(layer) + [
            pl.BlockSpec((None, 1, D_MODEL), lambda s: (layer, 0, 0)),
            _resident((D_MODEL, ODD_IN))(i),
        ],
        out_specs=[x_spec, hm, hm, hm],
        out_shape=[jax.ShapeDtypeStruct((SEQ, D_MODEL), F32), hs, hs, hs],
        compiler_params=_cparams("arbitrary"),
        name="odd_in",
    )(x, g1, wg, wu, wd, g, w)


KH = min(NA_KH, ROWS)
N_VARIANTS = KH
N_BIAS_ROWS = 2 * NA_KH - 1
C_R = 4
C_WIN = C_R + KH
C_KEYS = KH * GRID_W
assert KH % 2 == 0 and LANES == 2 * GRID_W


def _row_start(r):
    return jnp.clip(r - KH // 2, 0, ROWS - KH)


def _win_start(i):
    return jnp.clip(i * C_R - KH // 2, 0, ROWS - C_WIN)


def _attn_c_kernel(q_ref, k_ref, v_ref, bias_ref, o_ref):
    i = pl.program_id(0)
    base = _win_start(i)
    low = lax.broadcasted_iota(jnp.int32, (GRID_W, LANES), 1) < HEAD_DIM

    def row_scores(t):
        r = i * C_R + t
        rs = _row_start(r)
        koff = pl.multiple_of((rs - base) * GRID_W, GRID_W)
        first = rs - r + (NA_KH - 1)
        parts = []
        for pr in range(C_HEADS // 2):
            lanes = slice(pr * LANES, (pr + 1) * LANES)
            q = q_ref[t * GRID_W:(t + 1) * GRID_W, lanes]
            q2 = jnp.concatenate([jnp.where(low, q, 0), jnp.where(low, 0, q)], axis=0)
            parts.append(lax.dot_general(q2.astype(BF16), k_ref[pl.ds(koff, C_KEYS), lanes],
                                         (((1,), (1,)), ((), ())), preferred_element_type=F32))
        bias = jnp.concatenate(
            [jnp.concatenate([bias_ref[h, first + 2 * a] for a in range(KH // 2)], axis=1)
             for h in range(C_HEADS)], axis=0)
        return jnp.concatenate(parts, axis=0) + bias, koff

    nxt = row_scores(0)
    for t in range(C_R):
        s, koff = nxt
        if t + 1 < C_R:
            nxt = row_scores(t + 1)
        m = jnp.max(s, axis=1, keepdims=True)
        p = jnp.exp2(s - m)
        denom = jnp.sum(p, axis=1, keepdims=True)
        pb = p.astype(BF16)
        for pr in range(C_HEADS // 2):
            lanes = slice(pr * LANES, (pr + 1) * LANES)
            rows = slice(pr * LANES, (pr + 1) * LANES)
            o2 = jnp.dot(pb[rows], v_ref[pl.ds(koff, C_KEYS), lanes],
                         preferred_element_type=F32) / denom[rows]
            o_ref[t * GRID_W:(t + 1) * GRID_W, lanes] = jnp.where(
                low, o2[:GRID_W], o2[GRID_W:]).astype(BF16)


def _attn_c(q, k, v, bias):
    q_spec = pl.BlockSpec((C_R * GRID_W, C_WIDTH), lambda i: (i, 0))
    win_spec = pl.BlockSpec((pl.Element(C_WIN * GRID_W), pl.Element(C_WIDTH)),
                            lambda i: (_win_start(i) * GRID_W, 0))
    bias_spec = pl.BlockSpec(bias.shape, lambda i: (0, 0, 0, 0), pipeline_mode=pl.Buffered(1))
    return pl.pallas_call(
        _attn_c_kernel,
        grid=(ROWS // C_R,),
        in_specs=[q_spec, win_spec, win_spec, bias_spec],
        out_specs=q_spec,
        out_shape=jax.ShapeDtypeStruct((SEQ, C_WIDTH), BF16),
        compiler_params=_cparams("arbitrary"),
        name="attn_c",
    )(q, k, v, bias)


def _neighbourhood_bias(rel_bias):
    cols = np.arange(GRID_W)
    col_start = np.clip(cols - NA_KW // 2, 0, GRID_W - NA_KW)
    kc = np.arange(GRID_W)
    in_win = (kc[None, :] >= col_start[:, None]) & (kc[None, :] < col_start[:, None] + NA_KW)
    col_idx = np.clip(kc[None, :] - cols[:, None] + (NA_KW - 1), 0, 2 * NA_KW - 2)
    onehot = (col_idx[None] == np.arange(2 * NA_KW - 1)[:, None, None]).astype(np.float32)
    t = jnp.einsum('hej,jck->heck', rel_bias, jnp.asarray(onehot),
                   precision=lax.Precision.HIGHEST)
    t = jnp.where(in_win[None, None], t * LOG2E, NEG_INF)
    return jnp.concatenate([t[:, :-1], t[:, 1:]], axis=-1).astype(F32)


def kernel(x, ffn1_norm, ffn1_w_gate, ffn1_w_up, ffn1_w_down, mix_norm, ffn2_norm, ffn2_w_gate,
           ffn2_w_up, ffn2_w_down, even_w_in, a_q_norm, a_k_norm, b_sink, even_w_out, odd_w_qkv,
           c_rel_bias, odd_w_out, final_norm):
    assert x.shape == (1, SEQ, D_MODEL)
    xs = x.reshape(SEQ, D_MODEL)
    rope_tabs = _rope_factor_tables()
    reps = LANES // HEAD_DIM
    group_mean = jnp.asarray(
        np.kron(np.eye(reps), np.full((HEAD_DIM, HEAD_DIM), 1.0 / HEAD_DIM)), BF16)
    fg = final_norm.reshape(1, D_MODEL)
    vec = lambda v: v.reshape(DEPTH, 1, D_MODEL)
    g1, gmix, g2 = vec(ffn1_norm), vec(mix_norm), vec(ffn2_norm)
    w1 = (ffn1_w_gate, ffn1_w_up, ffn1_w_down)
    w2 = (ffn2_w_gate, ffn2_w_up, ffn2_w_down)
    w_in, w_out_e = even_w_in.astype(BF16), even_w_out.astype(BF16)
    w_qkv, w_out_o = odd_w_qkv.astype(BF16), odd_w_out.astype(BF16)

    for layer in range(DEPTH):
        i = layer // 2
        last = layer == DEPTH - 1
        if layer % 2 == 0:
            xs, qa, ka, va, qb, kb, vb = _even_in(
                xs, layer, i, g1, *w1, gmix, w_in,
                jnp.tile(a_q_norm[i], reps).reshape(1, LANES),
                jnp.tile(a_k_norm[i], reps).reshape(1, LANES), group_mean, rope_tabs)
            outs = [_attn_a(qa, ka, va), _attn_b(b_sink[i], qb, kb, vb)]
            w_out = w_out_e
        else:
            xs, q, k, v = _odd_in(xs, layer, i, g1, *w1, gmix, w_qkv)
            outs = [_attn_c(q, k, v, _neighbourhood_bias(c_rel_bias[i]))]
            w_out = w_out_o
        xs = _mix_out(xs, outs, layer, i, w_out, g2, *w2, fg, last)
    return xs.reshape(1, SEQ, D_MODEL)
```

```python
import functools

import jax
import jax.numpy as jnp
import numpy as np
from jax import lax
from jax.experimental import pallas as pl
from jax.experimental.pallas import tpu as pltpu

D_MODEL = 1024
SEQ = 16384
DEPTH = 2
HEAD_DIM = 64
A_HEADS = 8
A_KV = 2
B_HEADS = 8
B_KV = 2
C_HEADS = 16
D_FF = 2816
GRID_W = 64
ROWS = SEQ // GRID_W
Q_BLOCK = 128
WINDOW = 128
NA_KH = 8
NA_KW = 16
ROPE_THETA = 10000.0
EPS = 1e-6
EVEN_IN = (A_HEADS + 2 * A_KV + B_HEADS + 2 * B_KV) * HEAD_DIM
C_WIDTH = C_HEADS * HEAD_DIM
ODD_IN = 3 * C_WIDTH
NEG_INF = -1e30
LOG2E = float(np.log2(np.e))
QSCALE = HEAD_DIM ** -0.5 * LOG2E

LANES = 128
MXU_N = 256
GROUP = A_HEADS // A_KV
V_ROWS = HEAD_DIM + 16
KV_LANES = A_KV * HEAD_DIM
assert KV_LANES == LANES and B_KV == A_KV

F32 = jnp.float32
BF16 = jnp.bfloat16

VMEM_LIMIT = 56 * 1024 * 1024

TM = 512
TILE_ROWS = TM // GRID_W


def _cparams(*sem):
    return pltpu.CompilerParams(dimension_semantics=sem, vmem_limit_bytes=VMEM_LIMIT)


def _rms(x, g):
    ms = jnp.mean(x * x, axis=-1, keepdims=True)
    return x * lax.rsqrt(ms + EPS) * g


FFN_SPLITS = (0, 6 * MXU_N, D_FF)
assert D_FF % MXU_N == 0


def _ffn_apply(x, g, wg_ref, wu_ref, wd_ref):
    h = _rms(x, g).astype(BF16)
    acc = jnp.zeros_like(x)
    for lo, hi in zip(FFN_SPLITS[:-1], FFN_SPLITS[1:]):
        sl = slice(lo, hi)
        gate = jnp.dot(h, wg_ref[:, sl], preferred_element_type=F32)
        up = jnp.dot(h, wu_ref[:, sl], preferred_element_type=F32)
        act = (gate * jax.nn.sigmoid(gate) * up).astype(BF16)
        acc = acc + jnp.dot(act, wd_ref[sl, :], preferred_element_type=F32)
    return x + 0.5 * acc


N_CAST = 16
N_TILES = SEQ // TM
FFN_GRID = (N_CAST + N_TILES,)
FFN_SHAPES = [(D_MODEL, D_FF), (D_MODEL, D_FF), (D_FF, D_MODEL)]


def _tile(s):
    return jnp.maximum(s - N_CAST, 0)


def _weight_spec(shape, layer):
    rows, cols = shape
    assert rows % (16 * N_CAST) == 0
    return pl.BlockSpec((None, rows // N_CAST, cols),
                        lambda s: (layer, jnp.minimum(s, N_CAST - 1), 0))


def _weight_scratch(shapes):
    return [pltpu.VMEM(shape, BF16) for shape in shapes]


def _ffn_specs(layer):
    vec = pl.BlockSpec((None, 1, D_MODEL), lambda s: (layer, 0, 0))
    return [vec] + [_weight_spec(shape, layer) for shape in FFN_SHAPES]


def _cast_chunks(s, srcs, dsts):
    for src, dst in zip(srcs, dsts):
        n = src.shape[0]
        dst[pl.ds(pl.multiple_of(s * n, n), n), :] = src[...].astype(BF16)


def _split_heads_t(dst_ref, y, first_head, cols, ones_row=False):
    yt = y.T
    n = y.shape[0]
    for t in range(y.shape[1] // HEAD_DIM):
        dst_ref[first_head + t, 0:HEAD_DIM, cols] = yt[t * HEAD_DIM:(t + 1) * HEAD_DIM, :].astype(BF16)
        if ones_row:
            r = lax.broadcasted_iota(jnp.int32, (V_ROWS - HEAD_DIM, n), 0)
            dst_ref[first_head + t, HEAD_DIM:V_ROWS, cols] = jnp.where(r == 0, 1.0, 0.0).astype(BF16)


def _rope_factor_tables():
    lane = np.arange(LANES)
    d = lane % HEAD_DIM
    r = np.arange(ROWS, dtype=np.float64)[:, None]
    c = np.arange(GRID_W, dtype=np.float64)[:, None]
    inv_a = ROPE_THETA ** (-(2.0 * (d % 16)) / (HEAD_DIM // 2))
    sgn_a = np.where((d % 32) < 16, -1.0, 1.0)
    is_row = (d < 32)[None, :]
    a_tabs = [np.where(is_row, np.cos(r * inv_a), 0.0), np.where(is_row, sgn_a * np.sin(r * inv_a), 0.0),
              np.where(~is_row, np.cos(c * inv_a), 0.0), np.where(~is_row, sgn_a * np.sin(c * inv_a), 0.0)]
    inv_b = ROPE_THETA ** (-(2.0 * (d % 32)) / HEAD_DIM)
    sgn_b = np.where(d < 32, -1.0, 1.0)
    b_tabs = [np.cos(GRID_W * r * inv_b), sgn_b * np.sin(GRID_W * r * inv_b),
              np.cos(c * inv_b), sgn_b * np.sin(c * inv_b)]
    return [jnp.asarray(t, F32) for t in a_tabs + b_tabs]


def _rope_specs():
    row_tab = pl.BlockSpec((TILE_ROWS, LANES), lambda s: (_tile(s), 0))
    col_tab = pl.BlockSpec((GRID_W, LANES), lambda s: (0, 0))
    return [row_tab, row_tab, col_tab, col_tab] * 2


def _rope_tiles(grid_rows, arc, ars, acc_, acs, brc, brs, bcc, bcs):
    ca, sa, cb, sb = [], [], [], []
    for a in grid_rows:
        row = slice(a, a + 1)
        ca.append(arc[row, :] + acc_[...])
        sa.append(ars[row, :] + acs[...])
        rc, rs = brc[row, :], brs[row, :]
        cb.append(rc * bcc[...] - rs * bcs[...])
        sb.append(rs * bcc[...] + rc * bcs[...])
    cat = lambda parts: jnp.concatenate(parts, axis=0)
    return cat(ca), cat(sa), cat(cb), cat(sb)


EVEN_SPLIT = 2


def _even_in_kernel(x_ref, g1_ref, wg_ref, wu_ref, wd_ref, g_ref, w_ref, *refs):
    rest, (wg_sc, wu_sc, wd_sc, w_sc) = refs[:-4], refs[-4:]
    s = pl.program_id(0)

    @pl.when(s < N_CAST)
    def _():
        _cast_chunks(s, (wg_ref, wu_ref, wd_ref, w_ref), (wg_sc, wu_sc, wd_sc, w_sc))

    @pl.when(s >= N_CAST)
    def _():
        _even_in_tile(x_ref, g1_ref, wg_sc, wu_sc, wd_sc, g_ref, w_sc, *rest)


def _even_in_tile(x_ref, g1_ref, wg_ref, wu_ref, wd_ref, g_ref, w_ref, gq_ref, gk_ref, gm_ref,
                  arc, ars, acc_, acs, brc, brs, bcc, bcs,
                  x1_ref, qa_ref, ka_ref, va_ref, qb_ref, kb_ref, vb_ref):
    sub = TM // EVEN_SPLIT
    lane = lax.broadcasted_iota(jnp.int32, (sub, LANES), 1)
    gm = gm_ref[...]

    def head_norm(y, gain):
        sq = y * y
        hi = sq.astype(BF16)
        lo = (sq - hi.astype(F32)).astype(BF16)
        ms = (jnp.dot(hi, gm, preferred_element_type=F32)
              + jnp.dot(lo, gm, preferred_element_type=F32))
        return y * lax.rsqrt(ms + EPS) * gain

    def rope(y, c, s, half):
        fwd = pltpu.roll(y, LANES - half, 1)
        bwd = pltpu.roll(y, half, 1)
        partner = jnp.where((lane % (2 * half)) < half, fwd, bwd)
        return y * c + partner * s

    for part in range(EVEN_SPLIT):
        rows = slice(part * sub, (part + 1) * sub)
        x1 = _ffn_apply(x_ref[rows, :], g1_ref[...], wg_ref, wu_ref, wd_ref)
        x1_ref[rows, :] = x1
        h = _rms(x1, g_ref[...]).astype(BF16)
        proj = jnp.dot(h, w_ref[...], preferred_element_type=F32)
        grid_rows = range(part * sub // GRID_W, (part + 1) * sub // GRID_W)
        ca, sa, cb, sb = _rope_tiles(grid_rows, arc, ars, acc_, acs, brc, brs, bcc, bcs)
        col = 0
        for j in range(A_HEADS // 2):
            y = rope(head_norm(proj[:, col:col + LANES], gq_ref[...]), ca, sa, HEAD_DIM // 4)
            _split_heads_t(qa_ref, y * QSCALE, 2 * j, rows)
            col += LANES
        y = rope(head_norm(proj[:, col:col + LANES], gk_ref[...]), ca, sa, HEAD_DIM // 4)
        ka_ref[rows, :] = y.astype(BF16)
        col += LANES
        _split_heads_t(va_ref, proj[:, col:col + LANES], 0, rows, ones_row=True)
        col += LANES
        for j in range(B_HEADS // 2):
            y = rope(proj[:, col:col + LANES], cb, sb, HEAD_DIM // 2) * QSCALE
            _split_heads_t(qb_ref, y, 2 * j, rows)
            col += LANES
        y = rope(proj[:, col:col + LANES], cb, sb, HEAD_DIM // 2)
        kb_ref[rows, :] = y.astype(BF16)
        col += LANES
        _split_heads_t(vb_ref, proj[:, col:col + LANES], 0, rows, ones_row=True)


def _even_in(x, layer, i, g1, wg, wu, wd, g, w, gq, gk, gm, rope_tabs):
    const = lambda s: (0, 0)
    row = lambda s: (_tile(s), 0)
    k_spec = pl.BlockSpec((TM, KV_LANES), row)
    k_shape = jax.ShapeDtypeStruct((SEQ, KV_LANES), BF16)
    t_spec = lambda n, rows: pl.BlockSpec((n, rows, TM), lambda s: (0, 0, _tile(s)))
    t_shape = lambda n, rows: jax.ShapeDtypeStruct((n, rows, SEQ), BF16)
    x_spec = pl.BlockSpec((TM, D_MODEL), row)
    return pl.pallas_call(
        _even_in_kernel,
        grid=FFN_GRID,
        scratch_shapes=_weight_scratch(FFN_SHAPES + [(D_MODEL, EVEN_IN)]),
        in_specs=[x_spec] + _ffn_specs(layer) + [
            pl.BlockSpec((None, 1, D_MODEL), lambda s: (layer, 0, 0)),
            _weight_spec((D_MODEL, EVEN_IN), i),
            pl.BlockSpec((1, LANES), const),
            pl.BlockSpec((1, LANES), const),
            pl.BlockSpec((LANES, LANES), const),
        ] + _rope_specs(),
        out_specs=[x_spec, t_spec(A_HEADS, HEAD_DIM), k_spec, t_spec(A_KV, V_ROWS),
                   t_spec(B_HEADS, HEAD_DIM), k_spec, t_spec(B_KV, V_ROWS)],
        out_shape=[jax.ShapeDtypeStruct((SEQ, D_MODEL), F32),
                   t_shape(A_HEADS, HEAD_DIM), k_shape, t_shape(A_KV, V_ROWS),
                   t_shape(B_HEADS, HEAD_DIM), k_shape, t_shape(B_KV, V_ROWS)],
        compiler_params=_cparams("arbitrary"),
        name="even_in",
    )(x, g1, wg, wu, wd, g, w, gq, gk, gm, *rope_tabs)


A_TQ = 512
A_TK = 1024
A_UNROLL = 4
A_REF_KEYS = 128
A_MAX_SUM = 2.0 ** 100


def _attn_a_kernel(qt_ref, k_ref, vt_ref, o_ref, qt_sc, acc_sc):
    j = pl.program_id(0)
    cols = GROUP * A_TQ
    for jj in range(A_KV):
        rows = slice(jj * HEAD_DIM, (jj + 1) * HEAD_DIM)
        for g in range(GROUP):
            qt_sc[rows, g * A_TQ:(g + 1) * A_TQ] = jnp.where(j == jj, qt_ref[g], 0).astype(BF16)

    def scores(off, n=A_TK):
        return jnp.dot(k_ref[pl.ds(off, n), :], qt_sc[...], preferred_element_type=F32)

    def weighted_values(off, pt):
        return jnp.dot(vt_ref[:, pl.ds(off, A_TK)], pt, preferred_element_type=F32)

    def finish():
        acc = acc_sc[...]
        ot = acc[:HEAD_DIM] / acc[HEAD_DIM:HEAD_DIM + 1]
        head = lambda g: ot[:, g * A_TQ:(g + 1) * A_TQ]
        for p in range(GROUP // 2):
            pair = jnp.concatenate([head(2 * p), head(2 * p + 1)], axis=0)
            o_ref[:, p * LANES:(p + 1) * LANES] = pair.T.astype(BF16)

    m_ref = jnp.max(scores(0, A_REF_KEYS), axis=0, keepdims=True)
    acc_sc[...] = jnp.zeros((V_ROWS, cols), F32)

    def fast_body(c, carry):
        for u in range(A_UNROLL):
            off = pl.multiple_of((c * A_UNROLL + u) * A_TK, A_TK)
            pt = jnp.exp2(scores(off) - m_ref).astype(BF16)
            acc_sc[...] += weighted_values(off, pt)
        return carry

    lax.fori_loop(0, SEQ // (A_TK * A_UNROLL), fast_body, 0)
    lead_ok = jnp.max(jnp.abs(acc_sc[...])) < A_MAX_SUM

    @pl.when(lead_ok)
    def _():
        finish()

    @pl.when(jnp.logical_not(lead_ok))
    def _():
        acc_sc[...] = jnp.zeros((V_ROWS, cols), F32)

        def safe_body(c, m_prev):
            off = pl.multiple_of(c * A_TK, A_TK)
            st = scores(off)
            m_new = jnp.maximum(m_prev, jnp.max(st, axis=0, keepdims=True))
            alpha = jnp.exp2(m_prev - m_new)
            pt = jnp.exp2(st - m_new).astype(BF16)
            acc_sc[...] = acc_sc[...] * alpha + weighted_values(off, pt)
            return m_new

        lax.fori_loop(0, SEQ // A_TK, safe_body, jnp.full((1, cols), -jnp.inf, F32))
        finish()


def _attn_a(qt, k, vt):
    cols = GROUP * A_TQ
    return pl.pallas_call(
        _attn_a_kernel,
        grid=(A_KV, SEQ // A_TQ),
        in_specs=[
            pl.BlockSpec((GROUP, HEAD_DIM, A_TQ), lambda j, i: (j, 0, i)),
            pl.BlockSpec((SEQ, KV_LANES), lambda j, i: (0, 0)),
            pl.BlockSpec((None, V_ROWS, SEQ), lambda j, i: (j, 0, 0)),
        ],
        out_specs=pl.BlockSpec((A_TQ, GROUP * HEAD_DIM), lambda j, i: (i, j)),
        out_shape=jax.ShapeDtypeStruct((SEQ, A_HEADS * HEAD_DIM), BF16),
        scratch_shapes=[
            pltpu.VMEM((KV_LANES, cols), BF16),
            pltpu.VMEM((V_ROWS, cols), F32),
        ],
        compiler_params=_cparams("arbitrary", "arbitrary"),
        name="attn_a",
    )(qt, k, vt)


N_QBLK = SEQ // Q_BLOCK
B_TQ = 2 * Q_BLOCK
B_KEYS = B_TQ + 2 * Q_BLOCK


def _attn_b_kernel(sink_ref, qt_ref, kl_ref, kc_ref, kr_ref, vl_ref, vc_ref, vr_ref, o_ref):
    i = pl.program_id(0)
    k = jnp.concatenate([kl_ref[...], kc_ref[...], kr_ref[...]], axis=0)
    vts = [jnp.concatenate([vl_ref[j], vc_ref[j], vr_ref[j]], axis=1)
           for j in range(B_KV)]
    kj = lax.broadcasted_iota(jnp.int32, (B_KEYS, B_TQ), 0)
    qi = lax.broadcasted_iota(jnp.int32, (B_KEYS, B_TQ), 1)
    kpos = i * B_TQ - Q_BLOCK + kj
    valid = (jnp.abs(qi + Q_BLOCK - kj) <= WINDOW) & (kpos >= 0) & (kpos < SEQ)
    zeros = jnp.zeros((HEAD_DIM, GROUP * B_TQ), BF16)
    head_cols = lambda g: slice(g * B_TQ, (g + 1) * B_TQ)

    def scores(j):
        qt = jnp.concatenate([qt_ref[j * GROUP + g] for g in range(GROUP)], axis=1)
        parts = [qt if jj == j else zeros for jj in range(B_KV)]
        st = jnp.dot(k, jnp.concatenate(parts, axis=0), preferred_element_type=F32)
        return jnp.concatenate(
            [jnp.where(valid, st[:, head_cols(g)], NEG_INF) for g in range(GROUP)], axis=1)

    def attend(shift_by_max):
        res = []
        st_next = scores(0)
        for j in range(B_KV):
            sk = jnp.concatenate(
                [jnp.full((1, B_TQ), sink_ref[j * GROUP + g] * LOG2E, F32) for g in range(GROUP)],
                axis=1)
            st = st_next
            if j + 1 < B_KV:
                st_next = scores(j + 1)
            m = jnp.maximum(jnp.max(st, axis=0, keepdims=True), sk) if shift_by_max else sk
            pt = jnp.exp2(st - m).astype(BF16)
            pv = jnp.dot(vts[j], pt, preferred_element_type=F32)
            res.append((pv, jnp.exp2(sk - m)))
        return res

    def write(res):
        for j, (pv, sink) in enumerate(res):
            ot = pv[:HEAD_DIM] / (pv[HEAD_DIM:HEAD_DIM + 1] + sink)
            for p in range(GROUP // 2):
                pair = jnp.concatenate([ot[:, head_cols(2 * p)], ot[:, head_cols(2 * p + 1)]],
                                       axis=0)
                lanes = slice((j * GROUP // 2 + p) * LANES, (j * GROUP // 2 + p + 1) * LANES)
                o_ref[:, lanes] = pair.T.astype(BF16)

    fast = attend(shift_by_max=False)
    bound = functools.reduce(jnp.maximum, [jnp.max(jnp.abs(pv)) for pv, _ in fast])
    ok = bound < A_MAX_SUM

    @pl.when(ok)
    def _():
        write(fast)

    @pl.when(jnp.logical_not(ok))
    def _():
        write(attend(shift_by_max=True))


def _attn_b(sink, qt, k, vt):
    side = lambda d: (lambda i, s: jnp.clip(2 * i + d, 0, N_QBLK - 1))
    left, right = side(-1), side(2)
    k_side = lambda f: pl.BlockSpec((Q_BLOCK, KV_LANES), lambda i, s: (f(i, s), 0))
    v_side = lambda f: pl.BlockSpec((B_KV, V_ROWS, Q_BLOCK), lambda i, s: (0, 0, f(i, s)))
    return pl.pallas_call(
        _attn_b_kernel,
        grid_spec=pltpu.PrefetchScalarGridSpec(
            num_scalar_prefetch=1,
            grid=(SEQ // B_TQ,),
            in_specs=[
                pl.BlockSpec((B_HEADS, HEAD_DIM, B_TQ), lambda i, s: (0, 0, i)),
                k_side(left),
                pl.BlockSpec((B_TQ, KV_LANES), lambda i, s: (i, 0)),
                k_side(right),
                v_side(left),
                pl.BlockSpec((B_KV, V_ROWS, B_TQ), lambda i, s: (0, 0, i)),
                v_side(right),
            ],
            out_specs=pl.BlockSpec((B_TQ, B_HEADS * HEAD_DIM), lambda i, s: (i, 0)),
        ),
        out_shape=jax.ShapeDtypeStruct((SEQ, B_HEADS * HEAD_DIM), BF16),
        compiler_params=_cparams("arbitrary"),
        name="attn_b",
    )(sink, qt, k, k, k, vt, vt, vt)


def _mix_out_kernel(x_ref, *refs, n_in, final_norm):
    o_refs = refs[:n_in]
    w_ref, g2_ref, wg_ref, wu_ref, wd_ref, fg_ref, y_ref = refs[n_in:-4]
    w_sc, wg_sc, wu_sc, wd_sc = refs[-4:]
    s = pl.program_id(0)

    @pl.when(s < N_CAST)
    def _():
        _cast_chunks(s, (w_ref, wg_ref, wu_ref, wd_ref), (w_sc, wg_sc, wu_sc, wd_sc))

    @pl.when(s >= N_CAST)
    def _():
        o = jnp.concatenate([r[...] for r in o_refs], axis=-1)
        x2 = x_ref[...] + jnp.dot(o, w_sc[...], preferred_element_type=F32)
        y = _ffn_apply(x2, g2_ref[...], wg_sc, wu_sc, wd_sc)
        if final_norm:
            y = _rms(y, fg_ref[...])
        y_ref[...] = y


def _mix_out(x, outs, layer, i, w, g2, wg, wu, wd, fg, final_norm):
    row = lambda s: (_tile(s), 0)
    x_spec = pl.BlockSpec((TM, D_MODEL), row)
    in_specs = [x_spec]
    for o in outs:
        in_specs.append(pl.BlockSpec((TM, o.shape[1]), row))
    in_specs.append(_weight_spec((D_MODEL, D_MODEL), i))
    in_specs += _ffn_specs(layer)
    in_specs.append(pl.BlockSpec((1, D_MODEL), lambda s: (0, 0)))
    return pl.pallas_call(
        functools.partial(_mix_out_kernel, n_in=len(outs), final_norm=final_norm),
        grid=FFN_GRID,
        scratch_shapes=_weight_scratch([(D_MODEL, D_MODEL)] + FFN_SHAPES),
        in_specs=in_specs,
        out_specs=x_spec,
        out_shape=jax.ShapeDtypeStruct((SEQ, D_MODEL), F32),
        compiler_params=_cparams("arbitrary"),
        name="mix_out",
    )(x, *outs, w, g2, wg, wu, wd, fg)


def _odd_in_kernel(x_ref, g1_ref, wg_ref, wu_ref, wd_ref, g_ref, w_ref, x1_ref, q_ref, k_ref, v_ref,
                   wg_sc, wu_sc, wd_sc, w_sc):
    s = pl.program_id(0)

    @pl.when(s < N_CAST)
    def _():
        _cast_chunks(s, (wg_ref, wu_ref, wd_ref, w_ref), (wg_sc, wu_sc, wd_sc, w_sc))

    @pl.when(s >= N_CAST)
    def _():
        x1 = _ffn_apply(x_ref[...], g1_ref[...], wg_sc, wu_sc, wd_sc)
        x1_ref[...] = x1
        h = _rms(x1, g_ref[...]).astype(BF16)
        for idx, (dst, scale) in enumerate(((q_ref, QSCALE), (k_ref, None), (v_ref, None))):
            y = jnp.dot(h, w_sc[:, idx * C_WIDTH:(idx + 1) * C_WIDTH],
                        preferred_element_type=F32)
            if scale is not None:
                y = y * scale
            dst[...] = y.astype(BF16)


def _odd_in(x, layer, i, g1, wg, wu, wd, g, w):
    x_spec = pl.BlockSpec((TM, D_MODEL), lambda s: (_tile(s), 0))
    hm = pl.BlockSpec((TM, C_WIDTH), lambda s: (_tile(s), 0))
    hs = jax.ShapeDtypeStruct((SEQ, C_WIDTH), BF16)
    return pl.pallas_call(
        _odd_in_kernel,
        grid=FFN_GRID,
        scratch_shapes=_weight_scratch(FFN_SHAPES + [(D_MODEL, ODD_IN)]),
        in_specs=[x_spec] + _ffn_specs(layer) + [
            pl.BlockSpec((None, 1, D_MODEL), lambda s: (layer, 0, 0)),
            _weight_spec((D_MODEL, ODD_IN), i),
        ],
        out_specs=[x_spec, hm, hm, hm],
        out_shape=[jax.ShapeDtypeStruct((SEQ, D_MODEL), F32), hs, hs, hs],
        compiler_params=_cparams("arbitrary"),
        name="odd_in",
    )(x, g1, wg, wu, wd, g, w)


KH = min(NA_KH, ROWS)
N_VARIANTS = KH
N_BIAS_ROWS = 2 * NA_KH - 1
C_R = 4
C_WIN = C_R + KH
C_KEYS = KH * GRID_W
assert KH % 2 == 0 and LANES == 2 * GRID_W


def _row_start(r):
    return jnp.clip(r - KH // 2, 0, ROWS - KH)


def _win_start(i):
    return jnp.clip(i * C_R - KH // 2, 0, ROWS - C_WIN)


def _attn_c_kernel(q_ref, k_ref, v_ref, bias_ref, o_ref):
    i = pl.program_id(0)
    base = _win_start(i)
    low = lax.broadcasted_iota(jnp.int32, (GRID_W, LANES), 1) < HEAD_DIM

    def row_scores(t):
        r = i * C_R + t
        rs = _row_start(r)
        koff = pl.multiple_of((rs - base) * GRID_W, GRID_W)
        first = rs - r + (NA_KH - 1)
        parts = []
        for pr in range(C_HEADS // 2):
            lanes = slice(pr * LANES, (pr + 1) * LANES)
            q = q_ref[t * GRID_W:(t + 1) * GRID_W, lanes]
            q2 = jnp.concatenate([jnp.where(low, q, 0), jnp.where(low, 0, q)], axis=0)
            parts.append(lax.dot_general(q2.astype(BF16), k_ref[pl.ds(koff, C_KEYS), lanes],
                                         (((1,), (1,)), ((), ())), preferred_element_type=F32))
        bias = jnp.concatenate(
            [jnp.concatenate([bias_ref[h, first + 2 * a] for a in range(KH // 2)], axis=1)
             for h in range(C_HEADS)], axis=0)
        return jnp.concatenate(parts, axis=0) + bias, koff

    nxt = row_scores(0)
    for t in range(C_R):
        s, koff = nxt
        if t + 1 < C_R:
            nxt = row_scores(t + 1)
        m = jnp.max(s, axis=1, keepdims=True)
        p = jnp.exp2(s - m)
        denom = jnp.sum(p, axis=1, keepdims=True)
        pb = p.astype(BF16)
        for pr in range(C_HEADS // 2):
            lanes = slice(pr * LANES, (pr + 1) * LANES)
            rows = slice(pr * LANES, (pr + 1) * LANES)
            o2 = jnp.dot(pb[rows], v_ref[pl.ds(koff, C_KEYS), lanes],
                         preferred_element_type=F32) / denom[rows]
            o_ref[t * GRID_W:(t + 1) * GRID_W, lanes] = jnp.where(
                low, o2[:GRID_W], o2[GRID_W:]).astype(BF16)


def _attn_c(q, k, v, bias):
    q_spec = pl.BlockSpec((C_R * GRID_W, C_WIDTH), lambda i: (i, 0))
    win_spec = pl.BlockSpec((pl.Element(C_WIN * GRID_W), pl.Element(C_WIDTH)),
                            lambda i: (_win_start(i) * GRID_W, 0))
    bias_spec = pl.BlockSpec(bias.shape, lambda i: (0, 0, 0, 0), pipeline_mode=pl.Buffered(1))
    return pl.pallas_call(
        _attn_c_kernel,
        grid=(ROWS // C_R,),
        in_specs=[q_spec, win_spec, win_spec, bias_spec],
        out_specs=q_spec,
        out_shape=jax.ShapeDtypeStruct((SEQ, C_WIDTH), BF16),
        compiler_params=_cparams("arbitrary"),
        name="attn_c",
    )(q, k, v, bias)


def _neighbourhood_bias(rel_bias):
    cols = np.arange(GRID_W)
    col_start = np.clip(cols - NA_KW // 2, 0, GRID_W - NA_KW)
    kc = np.arange(GRID_W)
    in_win = (kc[None, :] >= col_start[:, None]) & (kc[None, :] < col_start[:, None] + NA_KW)
    n_rel = 2 * NA_KW - 1
    col_idx = np.where(in_win, kc[None, :] - cols[:, None] + (NA_KW - 1), n_rel)
    onehot = (col_idx[None] == np.arange(n_rel + 1)[:, None, None]).astype(np.float32)
    pair_onehot = np.zeros((2, n_rel + 1, GRID_W, 2, GRID_W), np.float32)
    pair_onehot[0, :, :, 0, :] = onehot
    pair_onehot[1, :, :, 1, :] = onehot
    pair_onehot = pair_onehot.reshape(2 * (n_rel + 1), GRID_W, 2 * GRID_W)
    table = jnp.concatenate(
        [rel_bias * LOG2E, jnp.full(rel_bias.shape[:2] + (1,), NEG_INF, F32)], axis=-1)
    table = jnp.concatenate([table[:, :-1], table[:, 1:]], axis=-1)
    return jnp.einsum('hem,mcn->hecn', table, jnp.asarray(pair_onehot),
                      precision=lax.Precision.HIGHEST)


def kernel(x, ffn1_norm, ffn1_w_gate, ffn1_w_up, ffn1_w_down, mix_norm, ffn2_norm, ffn2_w_gate,
           ffn2_w_up, ffn2_w_down, even_w_in, a_q_norm, a_k_norm, b_sink, even_w_out, odd_w_qkv,
           c_rel_bias, odd_w_out, final_norm):
    assert x.shape == (1, SEQ, D_MODEL)
    xs = x.reshape(SEQ, D_MODEL)
    rope_tabs = _rope_factor_tables()
    reps = LANES // HEAD_DIM
    group_mean = jnp.asarray(
        np.kron(np.eye(reps), np.full((HEAD_DIM, HEAD_DIM), 1.0 / HEAD_DIM)), BF16)
    fg = final_norm.reshape(1, D_MODEL)
    vec = lambda v: v.reshape(DEPTH, 1, D_MODEL)
    g1, gmix, g2 = vec(ffn1_norm), vec(mix_norm), vec(ffn2_norm)
    w1 = (ffn1_w_gate, ffn1_w_up, ffn1_w_down)
    w2 = (ffn2_w_gate, ffn2_w_up, ffn2_w_down)
    w_in, w_out_e, w_qkv, w_out_o = even_w_in, even_w_out, odd_w_qkv, odd_w_out

    for layer in range(DEPTH):
        i = layer // 2
        last = layer == DEPTH - 1
        if layer % 2 == 0:
            xs, qa, ka, va, qb, kb, vb = _even_in(
                xs, layer, i, g1, *w1, gmix, w_in,
                jnp.tile(a_q_norm[i], reps).reshape(1, LANES),
                jnp.tile(a_k_norm[i], reps).reshape(1, LANES), group_mean, rope_tabs)
            outs = [_attn_a(qa, ka, va), _attn_b(b_sink[i], qb, kb, vb)]
            w_out = w_out_e
        else:
            xs, q, k, v = _odd_in(xs, layer, i, g1, *w1, gmix, w_qkv)
            outs = [_attn_c(q, k, v, _neighbourhood_bias(c_rel_bias[i]))]
            w_out = w_out_o
        xs = _mix_out(xs, outs, layer, i, w_out, g2, *w2, fg, last)
    return xs.reshape(1, SEQ, D_MODEL)
```

```python
import functools

import jax
import jax.numpy as jnp
import numpy as np
from jax import lax
from jax.experimental import pallas as pl
from jax.experimental.pallas import tpu as pltpu

D_MODEL = 1024
SEQ = 16384
DEPTH = 2
HEAD_DIM = 64
A_HEADS = 8
A_KV = 2
B_HEADS = 8
B_KV = 2
C_HEADS = 16
D_FF = 2816
GRID_W = 64
ROWS = SEQ // GRID_W
Q_BLOCK = 128
WINDOW = 128
NA_KH = 8
NA_KW = 16
ROPE_THETA = 10000.0
EPS = 1e-6
EVEN_IN = (A_HEADS + 2 * A_KV + B_HEADS + 2 * B_KV) * HEAD_DIM
C_WIDTH = C_HEADS * HEAD_DIM
ODD_IN = 3 * C_WIDTH
NEG_INF = -1e30
LOG2E = float(np.log2(np.e))
QSCALE = HEAD_DIM ** -0.5 * LOG2E

LANES = 128
MXU_N = 256
GROUP = A_HEADS // A_KV
V_ROWS = HEAD_DIM + 16
KV_LANES = A_KV * HEAD_DIM
assert KV_LANES == LANES and B_KV == A_KV

F32 = jnp.float32
BF16 = jnp.bfloat16

VMEM_LIMIT = 56 * 1024 * 1024

TM = 512
TILE_ROWS = TM // GRID_W


def _cparams(*sem):
    return pltpu.CompilerParams(dimension_semantics=sem, vmem_limit_bytes=VMEM_LIMIT)


def _rms(x, g):
    ms = jnp.mean(x * x, axis=-1, keepdims=True)
    return x * lax.rsqrt(ms + EPS) * g


FFN_SPLITS = (0, 6 * MXU_N, D_FF)
assert D_FF % MXU_N == 0


def _ffn_apply(x, g, wg_ref, wu_ref, wd_ref):
    h = _rms(x, g).astype(BF16)
    acc = jnp.zeros_like(x)
    for lo, hi in zip(FFN_SPLITS[:-1], FFN_SPLITS[1:]):
        sl = slice(lo, hi)
        gate = jnp.dot(h, wg_ref[:, sl], preferred_element_type=F32)
        up = jnp.dot(h, wu_ref[:, sl], preferred_element_type=F32)
        act = (gate * jax.nn.sigmoid(gate) * up).astype(BF16)
        acc = acc + jnp.dot(act, wd_ref[sl, :], preferred_element_type=F32)
    return x + 0.5 * acc


N_CAST = 16
N_TILES = SEQ // TM
FFN_GRID = (N_CAST + N_TILES,)
FFN_SHAPES = [(D_MODEL, D_FF), (D_MODEL, D_FF), (D_FF, D_MODEL)]


def _tile(s):
    return jnp.maximum(s - N_CAST, 0)


def _weight_spec(shape, layer):
    rows, cols = shape
    assert rows % (16 * N_CAST) == 0
    return pl.BlockSpec((None, rows // N_CAST, cols),
                        lambda s: (layer, jnp.minimum(s, N_CAST - 1), 0))


def _weight_scratch(shapes):
    return [pltpu.VMEM(shape, BF16) for shape in shapes]


def _ffn_specs(layer):
    vec = pl.BlockSpec((None, 1, D_MODEL), lambda s: (layer, 0, 0))
    return [vec] + [_weight_spec(shape, layer) for shape in FFN_SHAPES]


def _cast_chunks(s, srcs, dsts):
    for src, dst in zip(srcs, dsts):
        n = src.shape[0]
        dst[pl.ds(pl.multiple_of(s * n, n), n), :] = src[...].astype(BF16)


def _split_heads_t(dst_ref, y, first_head, cols, ones_row=False):
    yt = y.T
    n = y.shape[0]
    for t in range(y.shape[1] // HEAD_DIM):
        dst_ref[first_head + t, 0:HEAD_DIM, cols] = yt[t * HEAD_DIM:(t + 1) * HEAD_DIM, :].astype(BF16)
        if ones_row:
            r = lax.broadcasted_iota(jnp.int32, (V_ROWS - HEAD_DIM, n), 0)
            dst_ref[first_head + t, HEAD_DIM:V_ROWS, cols] = jnp.where(r == 0, 1.0, 0.0).astype(BF16)


def _rope_factor_tables():
    lane = np.arange(LANES)
    d = lane % HEAD_DIM
    r = np.arange(ROWS, dtype=np.float64)[:, None]
    c = np.arange(GRID_W, dtype=np.float64)[:, None]
    inv_a = ROPE_THETA ** (-(2.0 * (d % 16)) / (HEAD_DIM // 2))
    sgn_a = np.where((d % 32) < 16, -1.0, 1.0)
    is_row = (d < 32)[None, :]
    a_tabs = [np.where(is_row, np.cos(r * inv_a), 0.0), np.where(is_row, sgn_a * np.sin(r * inv_a), 0.0),
              np.where(~is_row, np.cos(c * inv_a), 0.0), np.where(~is_row, sgn_a * np.sin(c * inv_a), 0.0)]
    inv_b = ROPE_THETA ** (-(2.0 * (d % 32)) / HEAD_DIM)
    sgn_b = np.where(d < 32, -1.0, 1.0)
    b_tabs = [np.cos(GRID_W * r * inv_b), sgn_b * np.sin(GRID_W * r * inv_b),
              np.cos(c * inv_b), sgn_b * np.sin(c * inv_b)]
    return [jnp.asarray(t, F32) for t in a_tabs + b_tabs]


def _rope_specs():
    row_tab = pl.BlockSpec((TILE_ROWS, LANES), lambda s: (_tile(s), 0))
    col_tab = pl.BlockSpec((GRID_W, LANES), lambda s: (0, 0))
    return [row_tab, row_tab, col_tab, col_tab] * 2


def _rope_tiles(grid_rows, arc, ars, acc_, acs, brc, brs, bcc, bcs):
    ca, sa, cb, sb = [], [], [], []
    for a in grid_rows:
        row = slice(a, a + 1)
        ca.append(arc[row, :] + acc_[...])
        sa.append(ars[row, :] + acs[...])
        rc, rs = brc[row, :], brs[row, :]
        cb.append(rc * bcc[...] - rs * bcs[...])
        sb.append(rs * bcc[...] + rc * bcs[...])
    cat = lambda parts: jnp.concatenate(parts, axis=0)
    return cat(ca), cat(sa), cat(cb), cat(sb)


EVEN_SPLIT = 2


def _even_in_kernel(x_ref, g1_ref, wg_ref, wu_ref, wd_ref, g_ref, w_ref, *refs):
    rest, (wg_sc, wu_sc, wd_sc, w_sc) = refs[:-4], refs[-4:]
    s = pl.program_id(0)

    @pl.when(s < N_CAST)
    def _():
        _cast_chunks(s, (wg_ref, wu_ref, wd_ref, w_ref), (wg_sc, wu_sc, wd_sc, w_sc))

    @pl.when(s >= N_CAST)
    def _():
        _even_in_tile(x_ref, g1_ref, wg_sc, wu_sc, wd_sc, g_ref, w_sc, *rest)


def _even_in_tile(x_ref, g1_ref, wg_ref, wu_ref, wd_ref, g_ref, w_ref, gq_ref, gk_ref, gm_ref,
                  arc, ars, acc_, acs, brc, brs, bcc, bcs,
                  x1_ref, qa_ref, ka_ref, va_ref, qb_ref, kb_ref, vb_ref):
    sub = TM // EVEN_SPLIT
    lane = lax.broadcasted_iota(jnp.int32, (sub, LANES), 1)
    gm = gm_ref[...]

    def head_norm(y, gain):
        sq = y * y
        hi = sq.astype(BF16)
        lo = (sq - hi.astype(F32)).astype(BF16)
        ms = jnp.dot(jnp.concatenate([hi, lo], axis=1), gm, preferred_element_type=F32)
        return y * lax.rsqrt(ms + EPS) * gain

    def rope(y, c, s, half):
        fwd = pltpu.roll(y, LANES - half, 1)
        bwd = pltpu.roll(y, half, 1)
        partner = jnp.where((lane % (2 * half)) < half, fwd, bwd)
        return y * c + partner * s

    for part in range(EVEN_SPLIT):
        rows = slice(part * sub, (part + 1) * sub)
        x1 = _ffn_apply(x_ref[rows, :], g1_ref[...], wg_ref, wu_ref, wd_ref)
        x1_ref[rows, :] = x1
        h = _rms(x1, g_ref[...]).astype(BF16)
        proj = jnp.dot(h, w_ref[...], preferred_element_type=F32)
        grid_rows = range(part * sub // GRID_W, (part + 1) * sub // GRID_W)
        ca, sa, cb, sb = _rope_tiles(grid_rows, arc, ars, acc_, acs, brc, brs, bcc, bcs)
        col = 0
        for j in range(A_HEADS // 2):
            y = rope(head_norm(proj[:, col:col + LANES], gq_ref[...]), ca, sa, HEAD_DIM // 4)
            _split_heads_t(qa_ref, y * QSCALE, 2 * j, rows)
            col += LANES
        y = rope(head_norm(proj[:, col:col + LANES], gk_ref[...]), ca, sa, HEAD_DIM // 4)
        ka_ref[rows, :] = y.astype(BF16)
        col += LANES
        _split_heads_t(va_ref, proj[:, col:col + LANES], 0, rows, ones_row=True)
        col += LANES
        for j in range(B_HEADS // 2):
            y = rope(proj[:, col:col + LANES], cb, sb, HEAD_DIM // 2) * QSCALE
            _split_heads_t(qb_ref, y, 2 * j, rows)
            col += LANES
        y = rope(proj[:, col:col + LANES], cb, sb, HEAD_DIM // 2)
        kb_ref[rows, :] = y.astype(BF16)
        col += LANES
        _split_heads_t(vb_ref, proj[:, col:col + LANES], 0, rows, ones_row=True)


def _even_in(x, layer, i, g1, wg, wu, wd, g, w, gq, gk, gm, rope_tabs):
    const = lambda s: (0, 0)
    row = lambda s: (_tile(s), 0)
    k_spec = pl.BlockSpec((TM, KV_LANES), row)
    k_shape = jax.ShapeDtypeStruct((SEQ, KV_LANES), BF16)
    t_spec = lambda n, rows: pl.BlockSpec((n, rows, TM), lambda s: (0, 0, _tile(s)))
    t_shape = lambda n, rows: jax.ShapeDtypeStruct((n, rows, SEQ), BF16)
    x_spec = pl.BlockSpec((TM, D_MODEL), row)
    return pl.pallas_call(
        _even_in_kernel,
        grid=FFN_GRID,
        scratch_shapes=_weight_scratch(FFN_SHAPES + [(D_MODEL, EVEN_IN)]),
        in_specs=[x_spec] + _ffn_specs(layer) + [
            pl.BlockSpec((None, 1, D_MODEL), lambda s: (layer, 0, 0)),
            _weight_spec((D_MODEL, EVEN_IN), i),
            pl.BlockSpec((1, LANES), const),
            pl.BlockSpec((1, LANES), const),
            pl.BlockSpec((2 * LANES, LANES), const),
        ] + _rope_specs(),
        out_specs=[x_spec, t_spec(A_HEADS, HEAD_DIM), k_spec, t_spec(A_KV, V_ROWS),
                   t_spec(B_HEADS, HEAD_DIM), k_spec, t_spec(B_KV, V_ROWS)],
        out_shape=[jax.ShapeDtypeStruct((SEQ, D_MODEL), F32),
                   t_shape(A_HEADS, HEAD_DIM), k_shape, t_shape(A_KV, V_ROWS),
                   t_shape(B_HEADS, HEAD_DIM), k_shape, t_shape(B_KV, V_ROWS)],
        compiler_params=_cparams("arbitrary"),
        name="even_in",
    )(x, g1, wg, wu, wd, g, w, gq, gk, gm, *rope_tabs)


A_TQ = 512
A_TK = 1024
A_UNROLL = 4
A_REF_KEYS = 128
A_MAX_SUM = 2.0 ** 100


def _attn_a_kernel(qt_ref, k_ref, vt_ref, o_ref, qt_sc, acc_sc):
    j = pl.program_id(0)
    cols = GROUP * A_TQ
    for jj in range(A_KV):
        rows = slice(jj * HEAD_DIM, (jj + 1) * HEAD_DIM)
        for g in range(GROUP):
            qt_sc[rows, g * A_TQ:(g + 1) * A_TQ] = jnp.where(j == jj, qt_ref[g], 0).astype(BF16)

    def scores(off, n=A_TK):
        return jnp.dot(k_ref[pl.ds(off, n), :], qt_sc[...], preferred_element_type=F32)

    def weighted_values(off, pt):
        return jnp.dot(vt_ref[:, pl.ds(off, A_TK)], pt, preferred_element_type=F32)

    def finish():
        acc = acc_sc[...]
        ot = acc[:HEAD_DIM] / acc[HEAD_DIM:HEAD_DIM + 1]
        head = lambda g: ot[:, g * A_TQ:(g + 1) * A_TQ]
        for p in range(GROUP // 2):
            pair = jnp.concatenate([head(2 * p), head(2 * p + 1)], axis=0)
            o_ref[:, p * LANES:(p + 1) * LANES] = pair.T.astype(BF16)

    m_ref = jnp.max(scores(0, A_REF_KEYS), axis=0, keepdims=True)
    acc_sc[...] = jnp.zeros((V_ROWS, cols), F32)

    def fast_body(c, carry):
        for u in range(A_UNROLL):
            off = pl.multiple_of((c * A_UNROLL + u) * A_TK, A_TK)
            pt = jnp.exp2(scores(off) - m_ref).astype(BF16)
            acc_sc[...] += weighted_values(off, pt)
        return carry

    lax.fori_loop(0, SEQ // (A_TK * A_UNROLL), fast_body, 0)
    lead_ok = jnp.max(jnp.abs(acc_sc[...])) < A_MAX_SUM

    @pl.when(lead_ok)
    def _():
        finish()

    @pl.when(jnp.logical_not(lead_ok))
    def _():
        acc_sc[...] = jnp.zeros((V_ROWS, cols), F32)

        def safe_body(c, m_prev):
            off = pl.multiple_of(c * A_TK, A_TK)
            st = scores(off)
            m_new = jnp.maximum(m_prev, jnp.max(st, axis=0, keepdims=True))
            alpha = jnp.exp2(m_prev - m_new)
            pt = jnp.exp2(st - m_new).astype(BF16)
            acc_sc[...] = acc_sc[...] * alpha + weighted_values(off, pt)
            return m_new

        lax.fori_loop(0, SEQ // A_TK, safe_body, jnp.full((1, cols), -jnp.inf, F32))
        finish()


def _attn_a(qt, k, vt):
    cols = GROUP * A_TQ
    return pl.pallas_call(
        _attn_a_kernel,
        grid=(A_KV, SEQ // A_TQ),
        in_specs=[
            pl.BlockSpec((GROUP, HEAD_DIM, A_TQ), lambda j, i: (j, 0, i)),
            pl.BlockSpec((SEQ, KV_LANES), lambda j, i: (0, 0)),
            pl.BlockSpec((None, V_ROWS, SEQ), lambda j, i: (j, 0, 0)),
        ],
        out_specs=pl.BlockSpec((A_TQ, GROUP * HEAD_DIM), lambda j, i: (i, j)),
        out_shape=jax.ShapeDtypeStruct((SEQ, A_HEADS * HEAD_DIM), BF16),
        scratch_shapes=[
            pltpu.VMEM((KV_LANES, cols), BF16),
            pltpu.VMEM((V_ROWS, cols), F32),
        ],
        compiler_params=_cparams("arbitrary", "arbitrary"),
        name="attn_a",
    )(qt, k, vt)


N_QBLK = SEQ // Q_BLOCK
B_TQ = 2 * Q_BLOCK
B_KEYS = B_TQ + 2 * Q_BLOCK


def _attn_b_kernel(sink_ref, qt_ref, kl_ref, kc_ref, kr_ref, vl_ref, vc_ref, vr_ref, o_ref):
    i = pl.program_id(0)
    k = jnp.concatenate([kl_ref[...], kc_ref[...], kr_ref[...]], axis=0)
    vts = [jnp.concatenate([vl_ref[j], vc_ref[j], vr_ref[j]], axis=1)
           for j in range(B_KV)]
    kj = lax.broadcasted_iota(jnp.int32, (B_KEYS, B_TQ), 0)
    qi = lax.broadcasted_iota(jnp.int32, (B_KEYS, B_TQ), 1)
    kpos = i * B_TQ - Q_BLOCK + kj
    valid = (jnp.abs(qi + Q_BLOCK - kj) <= WINDOW) & (kpos >= 0) & (kpos < SEQ)
    zeros = jnp.zeros((HEAD_DIM, GROUP * B_TQ), BF16)
    head_cols = lambda g: slice(g * B_TQ, (g + 1) * B_TQ)

    def scores(j):
        qt = jnp.concatenate([qt_ref[j * GROUP + g] for g in range(GROUP)], axis=1)
        parts = [qt if jj == j else zeros for jj in range(B_KV)]
        st = jnp.dot(k, jnp.concatenate(parts, axis=0), preferred_element_type=F32)
        return jnp.concatenate(
            [jnp.where(valid, st[:, head_cols(g)], NEG_INF) for g in range(GROUP)], axis=1)

    def attend(shift_by_max):
        res = []
        st_next = scores(0)
        for j in range(B_KV):
            sk = jnp.concatenate(
                [jnp.full((1, B_TQ), sink_ref[j * GROUP + g] * LOG2E, F32) for g in range(GROUP)],
                axis=1)
            st = st_next
            if j + 1 < B_KV:
                st_next = scores(j + 1)
            m = jnp.maximum(jnp.max(st, axis=0, keepdims=True), sk) if shift_by_max else sk
            pt = jnp.exp2(st - m).astype(BF16)
            pv = jnp.dot(vts[j], pt, preferred_element_type=F32)
            res.append((pv, jnp.exp2(sk - m)))
        return res

    def write(res):
        for j, (pv, sink) in enumerate(res):
            ot = pv[:HEAD_DIM] / (pv[HEAD_DIM:HEAD_DIM + 1] + sink)
            for p in range(GROUP // 2):
                pair = jnp.concatenate([ot[:, head_cols(2 * p)], ot[:, head_cols(2 * p + 1)]],
                                       axis=0)
                lanes = slice((j * GROUP // 2 + p) * LANES, (j * GROUP // 2 + p + 1) * LANES)
                o_ref[:, lanes] = pair.T.astype(BF16)

    fast = attend(shift_by_max=False)
    bound = functools.reduce(jnp.maximum, [jnp.max(jnp.abs(pv)) for pv, _ in fast])
    ok = bound < A_MAX_SUM

    @pl.when(ok)
    def _():
        write(fast)

    @pl.when(jnp.logical_not(ok))
    def _():
        write(attend(shift_by_max=True))


def _attn_b(sink, qt, k, vt):
    side = lambda d: (lambda i, s: jnp.clip(2 * i + d, 0, N_QBLK - 1))
    left, right = side(-1), side(2)
    k_side = lambda f: pl.BlockSpec((Q_BLOCK, KV_LANES), lambda i, s: (f(i, s), 0))
    v_side = lambda f: pl.BlockSpec((B_KV, V_ROWS, Q_BLOCK), lambda i, s: (0, 0, f(i, s)))
    return pl.pallas_call(
        _attn_b_kernel,
        grid_spec=pltpu.PrefetchScalarGridSpec(
            num_scalar_prefetch=1,
            grid=(SEQ // B_TQ,),
            in_specs=[
                pl.BlockSpec((B_HEADS, HEAD_DIM, B_TQ), lambda i, s: (0, 0, i)),
                k_side(left),
                pl.BlockSpec((B_TQ, KV_LANES), lambda i, s: (i, 0)),
                k_side(right),
                v_side(left),
                pl.BlockSpec((B_KV, V_ROWS, B_TQ), lambda i, s: (0, 0, i)),
                v_side(right),
            ],
            out_specs=pl.BlockSpec((B_TQ, B_HEADS * HEAD_DIM), lambda i, s: (i, 0)),
        ),
        out_shape=jax.ShapeDtypeStruct((SEQ, B_HEADS * HEAD_DIM), BF16),
        compiler_params=_cparams("arbitrary"),
        name="attn_b",
    )(sink, qt, k, k, k, vt, vt, vt)


def _mix_out_kernel(x_ref, *refs, n_in, final_norm):
    o_refs = refs[:n_in]
    w_ref, g2_ref, wg_ref, wu_ref, wd_ref, fg_ref, y_ref = refs[n_in:-4]
    w_sc, wg_sc, wu_sc, wd_sc = refs[-4:]
    s = pl.program_id(0)

    @pl.when(s < N_CAST)
    def _():
        _cast_chunks(s, (w_ref, wg_ref, wu_ref, wd_ref), (w_sc, wg_sc, wu_sc, wd_sc))

    @pl.when(s >= N_CAST)
    def _():
        o = jnp.concatenate([r[...] for r in o_refs], axis=-1)
        x2 = x_ref[...] + jnp.dot(o, w_sc[...], preferred_element_type=F32)
        y = _ffn_apply(x2, g2_ref[...], wg_sc, wu_sc, wd_sc)
        if final_norm:
            y = _rms(y, fg_ref[...])
        y_ref[...] = y


def _mix_out(x, outs, layer, i, w, g2, wg, wu, wd, fg, final_norm):
    row = lambda s: (_tile(s), 0)
    x_spec = pl.BlockSpec((TM, D_MODEL), row)
    in_specs = [x_spec]
    for o in outs:
        in_specs.append(pl.BlockSpec((TM, o.shape[1]), row))
    in_specs.append(_weight_spec((D_MODEL, D_MODEL), i))
    in_specs += _ffn_specs(layer)
    in_specs.append(pl.BlockSpec((1, D_MODEL), lambda s: (0, 0)))
    return pl.pallas_call(
        functools.partial(_mix_out_kernel, n_in=len(outs), final_norm=final_norm),
        grid=FFN_GRID,
        scratch_shapes=_weight_scratch([(D_MODEL, D_MODEL)] + FFN_SHAPES),
        in_specs=in_specs,
        out_specs=x_spec,
        out_shape=jax.ShapeDtypeStruct((SEQ, D_MODEL), F32),
        compiler_params=_cparams("arbitrary"),
        name="mix_out",
    )(x, *outs, w, g2, wg, wu, wd, fg)


def _odd_in_kernel(x_ref, g1_ref, wg_ref, wu_ref, wd_ref, g_ref, w_ref, x1_ref, q_ref, k_ref, v_ref,
                   wg_sc, wu_sc, wd_sc, w_sc):
    s = pl.program_id(0)

    @pl.when(s < N_CAST)
    def _():
        _cast_chunks(s, (wg_ref, wu_ref, wd_ref, w_ref), (wg_sc, wu_sc, wd_sc, w_sc))

    @pl.when(s >= N_CAST)
    def _():
        x1 = _ffn_apply(x_ref[...], g1_ref[...], wg_sc, wu_sc, wd_sc)
        x1_ref[...] = x1
        h = _rms(x1, g_ref[...]).astype(BF16)
        for idx, (dst, scale) in enumerate(((q_ref, QSCALE), (k_ref, None), (v_ref, None))):
            y = jnp.dot(h, w_sc[:, idx * C_WIDTH:(idx + 1) * C_WIDTH],
                        preferred_element_type=F32)
            if scale is not None:
                y = y * scale
            dst[...] = y.astype(BF16)


def _odd_in(x, layer, i, g1, wg, wu, wd, g, w):
    x_spec = pl.BlockSpec((TM, D_MODEL), lambda s: (_tile(s), 0))
    hm = pl.BlockSpec((TM, C_WIDTH), lambda s: (_tile(s), 0))
    hs = jax.ShapeDtypeStruct((SEQ, C_WIDTH), BF16)
    return pl.pallas_call(
        _odd_in_kernel,
        grid=FFN_GRID,
        scratch_shapes=_weight_scratch(FFN_SHAPES + [(D_MODEL, ODD_IN)]),
        in_specs=[x_spec] + _ffn_specs(layer) + [
            pl.BlockSpec((None, 1, D_MODEL), lambda s: (layer, 0, 0)),
            _weight_spec((D_MODEL, ODD_IN), i),
        ],
        out_specs=[x_spec, hm, hm, hm],
        out_shape=[jax.ShapeDtypeStruct((SEQ, D_MODEL), F32), hs, hs, hs],
        compiler_params=_cparams("arbitrary"),
        name="odd_in",
    )(x, g1, wg, wu, wd, g, w)


KH = min(NA_KH, ROWS)
C_R = 4
C_WIN = C_R + KH
C_KEYS = KH * GRID_W
assert KH % 2 == 0 and LANES == 2 * GRID_W


def _row_start(r):
    return jnp.clip(r - KH // 2, 0, ROWS - KH)


def _win_start(i):
    return jnp.clip(i * C_R - KH // 2, 0, ROWS - C_WIN)


def _attn_c_kernel(q_ref, k_ref, v_ref, bias_ref, o_ref):
    i = pl.program_id(0)
    base = _win_start(i)
    low = lax.broadcasted_iota(jnp.int32, (GRID_W, LANES), 1) < HEAD_DIM

    def row_scores(t):
        r = i * C_R + t
        rs = _row_start(r)
        koff = pl.multiple_of((rs - base) * GRID_W, GRID_W)
        first = rs - r + (NA_KH - 1)
        parts = []
        for pr in range(C_HEADS // 2):
            lanes = slice(pr * LANES, (pr + 1) * LANES)
            q = q_ref[t * GRID_W:(t + 1) * GRID_W, lanes]
            q2 = jnp.concatenate([jnp.where(low, q, 0), jnp.where(low, 0, q)], axis=0)
            parts.append(lax.dot_general(q2.astype(BF16), k_ref[pl.ds(koff, C_KEYS), lanes],
                                         (((1,), (1,)), ((), ())), preferred_element_type=F32))
        bias = jnp.concatenate(
            [jnp.concatenate([bias_ref[h, first + 2 * a] for a in range(KH // 2)], axis=1)
             for h in range(C_HEADS)], axis=0)
        return jnp.concatenate(parts, axis=0) + bias, koff

    nxt = row_scores(0)
    for t in range(C_R):
        s, koff = nxt
        if t + 1 < C_R:
            nxt = row_scores(t + 1)
        m = jnp.max(s, axis=1, keepdims=True)
        p = jnp.exp2(s - m)
        denom = jnp.sum(p, axis=1, keepdims=True)
        pb = p.astype(BF16)
        for pr in range(C_HEADS // 2):
            lanes = slice(pr * LANES, (pr + 1) * LANES)
            rows = slice(pr * LANES, (pr + 1) * LANES)
            o2 = jnp.dot(pb[rows], v_ref[pl.ds(koff, C_KEYS), lanes],
                         preferred_element_type=F32) / denom[rows]
            o_ref[t * GRID_W:(t + 1) * GRID_W, lanes] = jnp.where(
                low, o2[:GRID_W], o2[GRID_W:]).astype(BF16)


def _attn_c(q, k, v, bias):
    q_spec = pl.BlockSpec((C_R * GRID_W, C_WIDTH), lambda i: (i, 0))
    win_spec = pl.BlockSpec((pl.Element(C_WIN * GRID_W), pl.Element(C_WIDTH)),
                            lambda i: (_win_start(i) * GRID_W, 0))
    bias_spec = pl.BlockSpec(bias.shape, lambda i: (0, 0, 0, 0), pipeline_mode=pl.Buffered(1))
    return pl.pallas_call(
        _attn_c_kernel,
        grid=(ROWS // C_R,),
        in_specs=[q_spec, win_spec, win_spec, bias_spec],
        out_specs=q_spec,
        out_shape=jax.ShapeDtypeStruct((SEQ, C_WIDTH), BF16),
        compiler_params=_cparams("arbitrary"),
        name="attn_c",
    )(q, k, v, bias)


def _neighbourhood_bias(rel_bias):
    cols = np.arange(GRID_W)
    col_start = np.clip(cols - NA_KW // 2, 0, GRID_W - NA_KW)
    kc = np.arange(GRID_W)
    in_win = (kc[None, :] >= col_start[:, None]) & (kc[None, :] < col_start[:, None] + NA_KW)
    n_rel = 2 * NA_KW - 1
    col_idx = np.where(in_win, kc[None, :] - cols[:, None] + (NA_KW - 1), n_rel)
    onehot = (col_idx[None] == np.arange(n_rel + 1)[:, None, None]).astype(np.float32)
    pair_onehot = np.zeros((2, n_rel + 1, GRID_W, 2, GRID_W), np.float32)
    pair_onehot[0, :, :, 0, :] = onehot
    pair_onehot[1, :, :, 1, :] = onehot
    pair_onehot = pair_onehot.reshape(2 * (n_rel + 1), GRID_W, 2 * GRID_W)
    table = jnp.concatenate(
        [rel_bias * LOG2E, jnp.full(rel_bias.shape[:2] + (1,), NEG_INF, F32)], axis=-1)
    table = jnp.concatenate([table[:, :-1], table[:, 1:]], axis=-1)
    return jnp.einsum('hem,mcn->hecn', table, jnp.asarray(pair_onehot),
                      precision=lax.Precision.HIGHEST)


def kernel(x, ffn1_norm, ffn1_w_gate, ffn1_w_up, ffn1_w_down, mix_norm, ffn2_norm, ffn2_w_gate,
           ffn2_w_up, ffn2_w_down, even_w_in, a_q_norm, a_k_norm, b_sink, even_w_out, odd_w_qkv,
           c_rel_bias, odd_w_out, final_norm):
    assert x.shape == (1, SEQ, D_MODEL)
    xs = x.reshape(SEQ, D_MODEL)
    rope_tabs = _rope_factor_tables()
    reps = LANES // HEAD_DIM
    block_mean = np.kron(np.eye(reps), np.full((HEAD_DIM, HEAD_DIM), 1.0 / HEAD_DIM))
    group_mean = jnp.asarray(np.concatenate([block_mean, block_mean], axis=0), BF16)
    fg = final_norm.reshape(1, D_MODEL)
    vec = lambda v: v.reshape(DEPTH, 1, D_MODEL)
    g1, gmix, g2 = vec(ffn1_norm), vec(mix_norm), vec(ffn2_norm)
    w1 = (ffn1_w_gate, ffn1_w_up, ffn1_w_down)
    w2 = (ffn2_w_gate, ffn2_w_up, ffn2_w_down)

    for layer in range(DEPTH):
        i = layer // 2
        last = layer == DEPTH - 1
        if layer % 2 == 0:
            xs, qa, ka, va, qb, kb, vb = _even_in(
                xs, layer, i, g1, *w1, gmix, even_w_in,
                jnp.tile(a_q_norm[i], reps).reshape(1, LANES),
                jnp.tile(a_k_norm[i], reps).reshape(1, LANES), group_mean, rope_tabs)
            outs = [_attn_a(qa, ka, va), _attn_b(b_sink[i], qb, kb, vb)]
            w_out = even_w_out
        else:
            xs, q, k, v = _odd_in(xs, layer, i, g1, *w1, gmix, odd_w_qkv)
            outs = [_attn_c(q, k, v, _neighbourhood_bias(c_rel_bias[i]))]
            w_out = odd_w_out
        xs = _mix_out(xs, outs, layer, i, w_out, g2, *w2, fg, last)
    return xs.reshape(1, SEQ, D_MODEL)
```

```python
import functools

import jax
import jax.numpy as jnp
import numpy as np
from jax import lax
from jax.experimental import pallas as pl
from jax.experimental.pallas import tpu as pltpu

D_MODEL = 1024
SEQ = 16384
DEPTH = 2
HEAD_DIM = 64
A_HEADS = 8
A_KV = 2
B_HEADS = 8
B_KV = 2
C_HEADS = 16
D_FF = 2816
GRID_W = 64
ROWS = SEQ // GRID_W
Q_BLOCK = 128
WINDOW = 128
NA_KH = 8
NA_KW = 16
ROPE_THETA = 10000.0
EPS = 1e-6
EVEN_IN = (A_HEADS + 2 * A_KV + B_HEADS + 2 * B_KV) * HEAD_DIM
C_WIDTH = C_HEADS * HEAD_DIM
ODD_IN = 3 * C_WIDTH
NEG_INF = -1e30
LOG2E = float(np.log2(np.e))
QSCALE = HEAD_DIM ** -0.5 * LOG2E

LANES = 128
MXU_N = 256
GROUP = A_HEADS // A_KV
V_ROWS = HEAD_DIM + 16
KV_LANES = A_KV * HEAD_DIM
assert KV_LANES == LANES and B_KV == A_KV

F32 = jnp.float32
BF16 = jnp.bfloat16

VMEM_LIMIT = 56 * 1024 * 1024

TM = 512
TILE_ROWS = TM // GRID_W


def _cparams(*sem):
    return pltpu.CompilerParams(dimension_semantics=sem, vmem_limit_bytes=VMEM_LIMIT)


def _rms(x, g):
    ms = jnp.mean(x * x, axis=-1, keepdims=True)
    return x * lax.rsqrt(ms + EPS) * g


FFN_SPLITS = (0, 6 * MXU_N, D_FF)
assert D_FF % MXU_N == 0


def _ffn_apply(x, g, wg_ref, wu_ref, wd_ref):
    h = _rms(x, g).astype(BF16)
    acc = jnp.zeros_like(x)
    for lo, hi in zip(FFN_SPLITS[:-1], FFN_SPLITS[1:]):
        sl = slice(lo, hi)
        gate = jnp.dot(h, wg_ref[:, sl], preferred_element_type=F32)
        up = jnp.dot(h, wu_ref[:, sl], preferred_element_type=F32)
        act = (gate * jax.nn.sigmoid(gate) * up).astype(BF16)
        acc = acc + jnp.dot(act, wd_ref[sl, :], preferred_element_type=F32)
    return x + 0.5 * acc


N_CAST = 8
N_TILES = SEQ // TM
FFN_GRID = (N_CAST + N_TILES,)
FFN_SHAPES = [(D_MODEL, D_FF), (D_MODEL, D_FF), (D_FF, D_MODEL)]


def _tile(s):
    return jnp.maximum(s - N_CAST, 0)


def _weight_spec(shape, layer):
    rows, cols = shape
    assert rows % (16 * N_CAST) == 0
    return pl.BlockSpec((None, rows // N_CAST, cols),
                        lambda s: (layer, jnp.minimum(s, N_CAST - 1), 0))


def _weight_scratch(shapes):
    return [pltpu.VMEM(shape, BF16) for shape in shapes]


def _ffn_specs(layer):
    vec = pl.BlockSpec((None, 1, D_MODEL), lambda s: (layer, 0, 0))
    return [vec] + [_weight_spec(shape, layer) for shape in FFN_SHAPES]


def _cast_chunks(s, srcs, dsts):
    for src, dst in zip(srcs, dsts):
        n = src.shape[0]
        dst[pl.ds(pl.multiple_of(s * n, n), n), :] = src[...].astype(BF16)


def _split_heads_t(dst_ref, y, first_head, cols, ones_row=False):
    yt = y.T
    n = y.shape[0]
    for t in range(y.shape[1] // HEAD_DIM):
        dst_ref[first_head + t, 0:HEAD_DIM, cols] = yt[t * HEAD_DIM:(t + 1) * HEAD_DIM, :].astype(BF16)
        if ones_row:
            r = lax.broadcasted_iota(jnp.int32, (V_ROWS - HEAD_DIM, n), 0)
            dst_ref[first_head + t, HEAD_DIM:V_ROWS, cols] = jnp.where(r == 0, 1.0, 0.0).astype(BF16)


def _rope_factor_tables():
    lane = np.arange(LANES)
    d = lane % HEAD_DIM
    r = np.arange(ROWS, dtype=np.float64)[:, None]
    c = np.arange(GRID_W, dtype=np.float64)[:, None]
    inv_a = ROPE_THETA ** (-(2.0 * (d % 16)) / (HEAD_DIM // 2))
    sgn_a = np.where((d % 32) < 16, -1.0, 1.0)
    is_row = (d < 32)[None, :]
    a_tabs = [np.where(is_row, np.cos(r * inv_a), 0.0), np.where(is_row, sgn_a * np.sin(r * inv_a), 0.0),
              np.where(~is_row, np.cos(c * inv_a), 0.0), np.where(~is_row, sgn_a * np.sin(c * inv_a), 0.0)]
    inv_b = ROPE_THETA ** (-(2.0 * (d % 32)) / HEAD_DIM)
    sgn_b = np.where(d < 32, -1.0, 1.0)
    b_tabs = [np.cos(GRID_W * r * inv_b), sgn_b * np.sin(GRID_W * r * inv_b),
              np.cos(c * inv_b), sgn_b * np.sin(c * inv_b)]
    return [jnp.asarray(t, F32) for t in a_tabs + b_tabs]


def _rope_specs():
    row_tab = pl.BlockSpec((TILE_ROWS, LANES), lambda s: (_tile(s), 0))
    col_tab = pl.BlockSpec((GRID_W, LANES), lambda s: (0, 0))
    return [row_tab, row_tab, col_tab, col_tab] * 2


def _rope_tiles(grid_rows, arc, ars, acc_, acs, brc, brs, bcc, bcs):
    ca, sa, cb, sb = [], [], [], []
    for a in grid_rows:
        row = slice(a, a + 1)
        ca.append(arc[row, :] + acc_[...])
        sa.append(ars[row, :] + acs[...])
        rc, rs = brc[row, :], brs[row, :]
        cb.append(rc * bcc[...] - rs * bcs[...])
        sb.append(rs * bcc[...] + rc * bcs[...])
    cat = lambda parts: jnp.concatenate(parts, axis=0)
    return cat(ca), cat(sa), cat(cb), cat(sb)


EVEN_SPLIT = 2


def _even_in_kernel(x_ref, g1_ref, wg_ref, wu_ref, wd_ref, g_ref, w_ref, *refs):
    rest, (wg_sc, wu_sc, wd_sc, w_sc) = refs[:-4], refs[-4:]
    s = pl.program_id(0)

    @pl.when(s < N_CAST)
    def _():
        _cast_chunks(s, (wg_ref, wu_ref, wd_ref, w_ref), (wg_sc, wu_sc, wd_sc, w_sc))

    @pl.when(s >= N_CAST)
    def _():
        _even_in_tile(x_ref, g1_ref, wg_sc, wu_sc, wd_sc, g_ref, w_sc, *rest)


def _even_in_tile(x_ref, g1_ref, wg_ref, wu_ref, wd_ref, g_ref, w_ref, gq_ref, gk_ref, gm_ref,
                  arc, ars, acc_, acs, brc, brs, bcc, bcs,
                  x1_ref, qa_ref, ka_ref, va_ref, qb_ref, kb_ref, vb_ref):
    sub = TM // EVEN_SPLIT
    lane = lax.broadcasted_iota(jnp.int32, (sub, LANES), 1)
    gm = gm_ref[...]

    def head_norm(y, gain):
        sq = y * y
        hi = sq.astype(BF16)
        lo = (sq - hi.astype(F32)).astype(BF16)
        ms = jnp.dot(jnp.concatenate([hi, lo], axis=1), gm, preferred_element_type=F32)
        return y * lax.rsqrt(ms + EPS) * gain

    def rope(y, c, s, half):
        fwd = pltpu.roll(y, LANES - half, 1)
        bwd = pltpu.roll(y, half, 1)
        partner = jnp.where((lane % (2 * half)) < half, fwd, bwd)
        return y * c + partner * s

    for part in range(EVEN_SPLIT):
        rows = slice(part * sub, (part + 1) * sub)
        x1 = _ffn_apply(x_ref[rows, :], g1_ref[...], wg_ref, wu_ref, wd_ref)
        x1_ref[rows, :] = x1
        h = _rms(x1, g_ref[...]).astype(BF16)
        proj = jnp.dot(h, w_ref[...], preferred_element_type=F32)
        grid_rows = range(part * sub // GRID_W, (part + 1) * sub // GRID_W)
        ca, sa, cb, sb = _rope_tiles(grid_rows, arc, ars, acc_, acs, brc, brs, bcc, bcs)
        col = 0
        for j in range(A_HEADS // 2):
            y = rope(head_norm(proj[:, col:col + LANES], gq_ref[...]), ca, sa, HEAD_DIM // 4)
            _split_heads_t(qa_ref, y * QSCALE, 2 * j, rows)
            col += LANES
        y = rope(head_norm(proj[:, col:col + LANES], gk_ref[...]), ca, sa, HEAD_DIM // 4)
        ka_ref[rows, :] = y.astype(BF16)
        col += LANES
        _split_heads_t(va_ref, proj[:, col:col + LANES], 0, rows, ones_row=True)
        col += LANES
        for j in range(B_HEADS // 2):
            y = rope(proj[:, col:col + LANES], cb, sb, HEAD_DIM // 2) * QSCALE
            _split_heads_t(qb_ref, y, 2 * j, rows)
            col += LANES
        y = rope(proj[:, col:col + LANES], cb, sb, HEAD_DIM // 2)
        kb_ref[rows, :] = y.astype(BF16)
        col += LANES
        _split_heads_t(vb_ref, proj[:, col:col + LANES], 0, rows, ones_row=True)


def _even_in(x, layer, i, g1, wg, wu, wd, g, w, gq, gk, gm, rope_tabs):
    const = lambda s: (0, 0)
    row = lambda s: (_tile(s), 0)
    k_spec = pl.BlockSpec((TM, KV_LANES), row)
    k_shape = jax.ShapeDtypeStruct((SEQ, KV_LANES), BF16)
    t_spec = lambda n, rows: pl.BlockSpec((n, rows, TM), lambda s: (0, 0, _tile(s)))
    t_shape = lambda n, rows: jax.ShapeDtypeStruct((n, rows, SEQ), BF16)
    x_spec = pl.BlockSpec((TM, D_MODEL), row)
    return pl.pallas_call(
        _even_in_kernel,
        grid=FFN_GRID,
        scratch_shapes=_weight_scratch(FFN_SHAPES + [(D_MODEL, EVEN_IN)]),
        in_specs=[x_spec] + _ffn_specs(layer) + [
            pl.BlockSpec((None, 1, D_MODEL), lambda s: (layer, 0, 0)),
            _weight_spec((D_MODEL, EVEN_IN), i),
            pl.BlockSpec((1, LANES), const),
            pl.BlockSpec((1, LANES), const),
            pl.BlockSpec((2 * LANES, LANES), const),
        ] + _rope_specs(),
        out_specs=[x_spec, t_spec(A_HEADS, HEAD_DIM), k_spec, t_spec(A_KV, V_ROWS),
                   t_spec(B_HEADS, HEAD_DIM), k_spec, t_spec(B_KV, V_ROWS)],
        out_shape=[jax.ShapeDtypeStruct((SEQ, D_MODEL), F32),
                   t_shape(A_HEADS, HEAD_DIM), k_shape, t_shape(A_KV, V_ROWS),
                   t_shape(B_HEADS, HEAD_DIM), k_shape, t_shape(B_KV, V_ROWS)],
        compiler_params=_cparams("arbitrary"),
        name="even_in",
    )(x, g1, wg, wu, wd, g, w, gq, gk, gm, *rope_tabs)


A_TQ = 512
A_TK = 1024
A_UNROLL = 4
A_REF_KEYS = 16
A_MAX_SUM = 2.0 ** 100


def _attn_a_kernel(qt_ref, k_ref, vt_ref, o_ref, qt_sc, acc_sc):
    j = pl.program_id(0)
    cols = GROUP * A_TQ
    for jj in range(A_KV):
        rows = slice(jj * HEAD_DIM, (jj + 1) * HEAD_DIM)
        for g in range(GROUP):
            qt_sc[rows, g * A_TQ:(g + 1) * A_TQ] = jnp.where(j == jj, qt_ref[g], 0).astype(BF16)

    def scores(off, n=A_TK):
        return jnp.dot(k_ref[pl.ds(off, n), :], qt_sc[...], preferred_element_type=F32)

    def weighted_values(off, pt):
        return jnp.dot(vt_ref[:, pl.ds(off, A_TK)], pt, preferred_element_type=F32)

    def finish():
        acc = acc_sc[...]
        ot = acc[:HEAD_DIM] / acc[HEAD_DIM:HEAD_DIM + 1]
        head = lambda g: ot[:, g * A_TQ:(g + 1) * A_TQ]
        for p in range(GROUP // 2):
            pair = jnp.concatenate([head(2 * p), head(2 * p + 1)], axis=0)
            o_ref[:, p * LANES:(p + 1) * LANES] = pair.T.astype(BF16)

    m_ref = jnp.max(scores(0, A_REF_KEYS), axis=0, keepdims=True)
    acc_sc[...] = jnp.zeros((V_ROWS, cols), F32)

    def fast_body(c, carry):
        for u in range(A_UNROLL):
            off = pl.multiple_of((c * A_UNROLL + u) * A_TK, A_TK)
            pt = jnp.exp2(scores(off) - m_ref).astype(BF16)
            acc_sc[...] += weighted_values(off, pt)
        return carry

    lax.fori_loop(0, SEQ // (A_TK * A_UNROLL), fast_body, 0)
    lead_ok = jnp.max(jnp.abs(acc_sc[...])) < A_MAX_SUM

    @pl.when(lead_ok)
    def _():
        finish()

    @pl.when(jnp.logical_not(lead_ok))
    def _():
        acc_sc[...] = jnp.zeros((V_ROWS, cols), F32)

        def safe_body(c, m_prev):
            off = pl.multiple_of(c * A_TK, A_TK)
            st = scores(off)
            m_new = jnp.maximum(m_prev, jnp.max(st, axis=0, keepdims=True))
            alpha = jnp.exp2(m_prev - m_new)
            pt = jnp.exp2(st - m_new).astype(BF16)
            acc_sc[...] = acc_sc[...] * alpha + weighted_values(off, pt)
            return m_new

        lax.fori_loop(0, SEQ // A_TK, safe_body, jnp.full((1, cols), -jnp.inf, F32))
        finish()


def _attn_a(qt, k, vt):
    cols = GROUP * A_TQ
    return pl.pallas_call(
        _attn_a_kernel,
        grid=(A_KV, SEQ // A_TQ),
        in_specs=[
            pl.BlockSpec((GROUP, HEAD_DIM, A_TQ), lambda j, i: (j, 0, i)),
            pl.BlockSpec((SEQ, KV_LANES), lambda j, i: (0, 0)),
            pl.BlockSpec((None, V_ROWS, SEQ), lambda j, i: (j, 0, 0)),
        ],
        out_specs=pl.BlockSpec((A_TQ, GROUP * HEAD_DIM), lambda j, i: (i, j)),
        out_shape=jax.ShapeDtypeStruct((SEQ, A_HEADS * HEAD_DIM), BF16),
        scratch_shapes=[
            pltpu.VMEM((KV_LANES, cols), BF16),
            pltpu.VMEM((V_ROWS, cols), F32),
        ],
        compiler_params=_cparams("arbitrary", "arbitrary"),
        name="attn_a",
    )(qt, k, vt)


N_QBLK = SEQ // Q_BLOCK
B_TQ = 2 * Q_BLOCK
B_KEYS = B_TQ + 2 * Q_BLOCK


def _attn_b_kernel(sink_ref, qt_ref, kl_ref, kc_ref, kr_ref, vl_ref, vc_ref, vr_ref, o_ref):
    i = pl.program_id(0)
    k = jnp.concatenate([kl_ref[...], kc_ref[...], kr_ref[...]], axis=0)
    vts = [jnp.concatenate([vl_ref[j], vc_ref[j], vr_ref[j]], axis=1)
           for j in range(B_KV)]
    kj = lax.broadcasted_iota(jnp.int32, (B_KEYS, B_TQ), 0)
    qi = lax.broadcasted_iota(jnp.int32, (B_KEYS, B_TQ), 1)
    kpos = i * B_TQ - Q_BLOCK + kj
    valid = (jnp.abs(qi + Q_BLOCK - kj) <= WINDOW) & (kpos >= 0) & (kpos < SEQ)
    zeros = jnp.zeros((HEAD_DIM, GROUP * B_TQ), BF16)
    head_cols = lambda g: slice(g * B_TQ, (g + 1) * B_TQ)

    def scores(j):
        qt = jnp.concatenate([qt_ref[j * GROUP + g] for g in range(GROUP)], axis=1)
        parts = [qt if jj == j else zeros for jj in range(B_KV)]
        st = jnp.dot(k, jnp.concatenate(parts, axis=0), preferred_element_type=F32)
        return jnp.concatenate(
            [jnp.where(valid, st[:, head_cols(g)], NEG_INF) for g in range(GROUP)], axis=1)

    def attend(shift_by_max):
        res = []
        st_next = scores(0)
        for j in range(B_KV):
            sk = jnp.concatenate(
                [jnp.full((1, B_TQ), sink_ref[j * GROUP + g] * LOG2E, F32) for g in range(GROUP)],
                axis=1)
            st = st_next
            if j + 1 < B_KV:
                st_next = scores(j + 1)
            m = jnp.maximum(jnp.max(st, axis=0, keepdims=True), sk) if shift_by_max else sk
            pt = jnp.exp2(st - m).astype(BF16)
            pv = jnp.dot(vts[j], pt, preferred_element_type=F32)
            res.append((pv, jnp.exp2(sk - m)))
        return res

    def write(res):
        for j, (pv, sink) in enumerate(res):
            ot = pv[:HEAD_DIM] / (pv[HEAD_DIM:HEAD_DIM + 1] + sink)
            for p in range(GROUP // 2):
                pair = jnp.concatenate([ot[:, head_cols(2 * p)], ot[:, head_cols(2 * p + 1)]],
                                       axis=0)
                lanes = slice((j * GROUP // 2 + p) * LANES, (j * GROUP // 2 + p + 1) * LANES)
                o_ref[:, lanes] = pair.T.astype(BF16)

    fast = attend(shift_by_max=False)
    bound = functools.reduce(jnp.maximum, [jnp.max(jnp.abs(pv)) for pv, _ in fast])
    ok = bound < A_MAX_SUM

    @pl.when(ok)
    def _():
        write(fast)

    @pl.when(jnp.logical_not(ok))
    def _():
        write(attend(shift_by_max=True))


def _attn_b(sink, qt, k, vt):
    per_step = B_TQ // Q_BLOCK
    side = lambda d: (lambda i, s: jnp.clip(per_step * i + d, 0, N_QBLK - 1))
    left, right = side(-1), side(per_step)
    k_side = lambda f: pl.BlockSpec((Q_BLOCK, KV_LANES), lambda i, s: (f(i, s), 0))
    v_side = lambda f: pl.BlockSpec((B_KV, V_ROWS, Q_BLOCK), lambda i, s: (0, 0, f(i, s)))
    return pl.pallas_call(
        _attn_b_kernel,
        grid_spec=pltpu.PrefetchScalarGridSpec(
            num_scalar_prefetch=1,
            grid=(SEQ // B_TQ,),
            in_specs=[
                pl.BlockSpec((B_HEADS, HEAD_DIM, B_TQ), lambda i, s: (0, 0, i)),
                k_side(left),
                pl.BlockSpec((B_TQ, KV_LANES), lambda i, s: (i, 0)),
                k_side(right),
                v_side(left),
                pl.BlockSpec((B_KV, V_ROWS, B_TQ), lambda i, s: (0, 0, i)),
                v_side(right),
            ],
            out_specs=pl.BlockSpec((B_TQ, B_HEADS * HEAD_DIM), lambda i, s: (i, 0)),
        ),
        out_shape=jax.ShapeDtypeStruct((SEQ, B_HEADS * HEAD_DIM), BF16),
        compiler_params=_cparams("arbitrary"),
        name="attn_b",
    )(sink, qt, k, k, k, vt, vt, vt)


def _mix_out_kernel(x_ref, *refs, n_in, final_norm):
    o_refs = refs[:n_in]
    w_ref, g2_ref, wg_ref, wu_ref, wd_ref, fg_ref, y_ref = refs[n_in:-4]
    w_sc, wg_sc, wu_sc, wd_sc = refs[-4:]
    s = pl.program_id(0)

    @pl.when(s < N_CAST)
    def _():
        _cast_chunks(s, (w_ref, wg_ref, wu_ref, wd_ref), (w_sc, wg_sc, wu_sc, wd_sc))

    @pl.when(s >= N_CAST)
    def _():
        o = jnp.concatenate([r[...] for r in o_refs], axis=-1)
        x2 = x_ref[...] + jnp.dot(o, w_sc[...], preferred_element_type=F32)
        y = _ffn_apply(x2, g2_ref[...], wg_sc, wu_sc, wd_sc)
        if final_norm:
            y = _rms(y, fg_ref[...])
        y_ref[...] = y


def _mix_out(x, outs, layer, i, w, g2, wg, wu, wd, fg, final_norm):
    row = lambda s: (_tile(s), 0)
    x_spec = pl.BlockSpec((TM, D_MODEL), row)
    in_specs = [x_spec]
    for o in outs:
        in_specs.append(pl.BlockSpec((TM, o.shape[1]), row))
    in_specs.append(_weight_spec((D_MODEL, D_MODEL), i))
    in_specs += _ffn_specs(layer)
    in_specs.append(pl.BlockSpec((1, D_MODEL), lambda s: (0, 0)))
    return pl.pallas_call(
        functools.partial(_mix_out_kernel, n_in=len(outs), final_norm=final_norm),
        grid=FFN_GRID,
        scratch_shapes=_weight_scratch([(D_MODEL, D_MODEL)] + FFN_SHAPES),
        in_specs=in_specs,
        out_specs=x_spec,
        out_shape=jax.ShapeDtypeStruct((SEQ, D_MODEL), F32),
        compiler_params=_cparams("arbitrary"),
        name="mix_out",
    )(x, *outs, w, g2, wg, wu, wd, fg)


def _odd_in_kernel(x_ref, g1_ref, wg_ref, wu_ref, wd_ref, g_ref, w_ref, x1_ref, q_ref, k_ref, v_ref,
                   wg_sc, wu_sc, wd_sc, w_sc):
    s = pl.program_id(0)

    @pl.when(s < N_CAST)
    def _():
        _cast_chunks(s, (wg_ref, wu_ref, wd_ref, w_ref), (wg_sc, wu_sc, wd_sc, w_sc))

    @pl.when(s >= N_CAST)
    def _():
        x1 = _ffn_apply(x_ref[...], g1_ref[...], wg_sc, wu_sc, wd_sc)
        x1_ref[...] = x1
        h = _rms(x1, g_ref[...]).astype(BF16)
        for idx, (dst, scale) in enumerate(((q_ref, QSCALE), (k_ref, None), (v_ref, None))):
            y = jnp.dot(h, w_sc[:, idx * C_WIDTH:(idx + 1) * C_WIDTH],
                        preferred_element_type=F32)
            if scale is not None:
                y = y * scale
            dst[...] = y.astype(BF16)


def _odd_in(x, layer, i, g1, wg, wu, wd, g, w):
    x_spec = pl.BlockSpec((TM, D_MODEL), lambda s: (_tile(s), 0))
    hm = pl.BlockSpec((TM, C_WIDTH), lambda s: (_tile(s), 0))
    hs = jax.ShapeDtypeStruct((SEQ, C_WIDTH), BF16)
    return pl.pallas_call(
        _odd_in_kernel,
        grid=FFN_GRID,
        scratch_shapes=_weight_scratch(FFN_SHAPES + [(D_MODEL, ODD_IN)]),
        in_specs=[x_spec] + _ffn_specs(layer) + [
            pl.BlockSpec((None, 1, D_MODEL), lambda s: (layer, 0, 0)),
            _weight_spec((D_MODEL, ODD_IN), i),
        ],
        out_specs=[x_spec, hm, hm, hm],
        out_shape=[jax.ShapeDtypeStruct((SEQ, D_MODEL), F32), hs, hs, hs],
        compiler_params=_cparams("arbitrary"),
        name="odd_in",
    )(x, g1, wg, wu, wd, g, w)


KH = min(NA_KH, ROWS)
C_R = 8
C_WIN = C_R + KH
C_KEYS = KH * GRID_W
assert KH % 2 == 0 and LANES == 2 * GRID_W


def _row_start(r):
    return jnp.clip(r - KH // 2, 0, ROWS - KH)


def _win_start(i):
    return jnp.clip(i * C_R - KH // 2, 0, ROWS - C_WIN)


def _attn_c_kernel(q_ref, k_ref, v_ref, bias_ref, o_ref):
    i = pl.program_id(0)
    base = _win_start(i)
    low = lax.broadcasted_iota(jnp.int32, (GRID_W, LANES), 1) < HEAD_DIM

    def row_scores(t):
        r = i * C_R + t
        rs = _row_start(r)
        koff = pl.multiple_of((rs - base) * GRID_W, GRID_W)
        first = rs - r + (NA_KH - 1)
        parts = []
        for pr in range(C_HEADS // 2):
            lanes = slice(pr * LANES, (pr + 1) * LANES)
            q = q_ref[t * GRID_W:(t + 1) * GRID_W, lanes]
            q2 = jnp.concatenate([jnp.where(low, q, 0), jnp.where(low, 0, q)], axis=0)
            parts.append(lax.dot_general(q2.astype(BF16), k_ref[pl.ds(koff, C_KEYS), lanes],
                                         (((1,), (1,)), ((), ())), preferred_element_type=F32))
        bias = jnp.concatenate(
            [jnp.concatenate([bias_ref[h, first + 2 * a] for a in range(KH // 2)], axis=1)
             for h in range(C_HEADS)], axis=0)
        return jnp.concatenate(parts, axis=0) + bias, koff

    nxt = row_scores(0)
    for t in range(C_R):
        s, koff = nxt
        if t + 1 < C_R:
            nxt = row_scores(t + 1)
        m = jnp.max(s, axis=1, keepdims=True)
        p = jnp.exp2(s - m)
        denom = jnp.sum(p, axis=1, keepdims=True)
        pb = p.astype(BF16)
        for pr in range(C_HEADS // 2):
            lanes = slice(pr * LANES, (pr + 1) * LANES)
            rows = slice(pr * LANES, (pr + 1) * LANES)
            o2 = jnp.dot(pb[rows], v_ref[pl.ds(koff, C_KEYS), lanes],
                         preferred_element_type=F32) / denom[rows]
            o_ref[t * GRID_W:(t + 1) * GRID_W, lanes] = jnp.where(
                low, o2[:GRID_W], o2[GRID_W:]).astype(BF16)


def _attn_c(q, k, v, bias):
    q_spec = pl.BlockSpec((C_R * GRID_W, C_WIDTH), lambda i: (i, 0))
    win_spec = pl.BlockSpec((pl.Element(C_WIN * GRID_W), pl.Element(C_WIDTH)),
                            lambda i: (_win_start(i) * GRID_W, 0))
    bias_spec = pl.BlockSpec(bias.shape, lambda i: (0, 0, 0, 0), pipeline_mode=pl.Buffered(1))
    return pl.pallas_call(
        _attn_c_kernel,
        grid=(ROWS // C_R,),
        in_specs=[q_spec, win_spec, win_spec, bias_spec],
        out_specs=q_spec,
        out_shape=jax.ShapeDtypeStruct((SEQ, C_WIDTH), BF16),
        compiler_params=_cparams("arbitrary"),
        name="attn_c",
    )(q, k, v, bias)


def _neighbourhood_bias(rel_bias):
    cols = np.arange(GRID_W)
    col_start = np.clip(cols - NA_KW // 2, 0, GRID_W - NA_KW)
    kc = np.arange(GRID_W)
    in_win = (kc[None, :] >= col_start[:, None]) & (kc[None, :] < col_start[:, None] + NA_KW)
    n_rel = 2 * NA_KW - 1
    col_idx = np.where(in_win, kc[None, :] - cols[:, None] + (NA_KW - 1), n_rel)
    onehot = (col_idx[None] == np.arange(n_rel + 1)[:, None, None]).astype(np.float32)
    pair_onehot = np.zeros((2, n_rel + 1, GRID_W, 2, GRID_W), np.float32)
    pair_onehot[0, :, :, 0, :] = onehot
    pair_onehot[1, :, :, 1, :] = onehot
    pair_onehot = pair_onehot.reshape(2 * (n_rel + 1), GRID_W, 2 * GRID_W)
    table = jnp.concatenate(
        [rel_bias * LOG2E, jnp.full(rel_bias.shape[:2] + (1,), NEG_INF, F32)], axis=-1)
    table = jnp.concatenate([table[:, :-1], table[:, 1:]], axis=-1)
    return jnp.einsum('hem,mcn->hecn', table, jnp.asarray(pair_onehot),
                      precision=lax.Precision.HIGHEST)


def kernel(x, ffn1_norm, ffn1_w_gate, ffn1_w_up, ffn1_w_down, mix_norm, ffn2_norm, ffn2_w_gate,
           ffn2_w_up, ffn2_w_down, even_w_in, a_q_norm, a_k_norm, b_sink, even_w_out, odd_w_qkv,
           c_rel_bias, odd_w_out, final_norm):
    assert x.shape == (1, SEQ, D_MODEL)
    xs = x.reshape(SEQ, D_MODEL)
    rope_tabs = _rope_factor_tables()
    reps = LANES // HEAD_DIM
    block_mean = np.kron(np.eye(reps), np.full((HEAD_DIM, HEAD_DIM), 1.0 / HEAD_DIM))
    group_mean = jnp.asarray(np.concatenate([block_mean, block_mean], axis=0), BF16)
    fg = final_norm.reshape(1, D_MODEL)
    vec = lambda v: v.reshape(DEPTH, 1, D_MODEL)
    g1, gmix, g2 = vec(ffn1_norm), vec(mix_norm), vec(ffn2_norm)
    w1 = (ffn1_w_gate, ffn1_w_up, ffn1_w_down)
    w2 = (ffn2_w_gate, ffn2_w_up, ffn2_w_down)

    for layer in range(DEPTH):
        i = layer // 2
        last = layer == DEPTH - 1
        if layer % 2 == 0:
            xs, qa, ka, va, qb, kb, vb = _even_in(
                xs, layer, i, g1, *w1, gmix, even_w_in,
                jnp.tile(a_q_norm[i], reps).reshape(1, LANES),
                jnp.tile(a_k_norm[i], reps).reshape(1, LANES), group_mean, rope_tabs)
            outs = [_attn_a(qa, ka, va), _attn_b(b_sink[i], qb, kb, vb)]
            w_out = even_w_out
        else:
            xs, q, k, v = _odd_in(xs, layer, i, g1, *w1, gmix, odd_w_qkv)
            outs = [_attn_c(q, k, v, _neighbourhood_bias(c_rel_bias[i]))]
            w_out = odd_w_out
        xs = _mix_out(xs, outs, layer, i, w_out, g2, *w2, fg, last)
    return xs.reshape(1, SEQ, D_MODEL)
```

```python
import functools

import jax
import jax.numpy as jnp
import numpy as np
from jax import lax
from jax.experimental import pallas as pl
from jax.experimental.pallas import tpu as pltpu

D_MODEL = 1024
SEQ = 16384
DEPTH = 2
HEAD_DIM = 64
A_HEADS = 8
A_KV = 2
B_HEADS = 8
B_KV = 2
C_HEADS = 16
D_FF = 2816
GRID_W = 64
ROWS = SEQ // GRID_W
Q_BLOCK = 128
WINDOW = 128
NA_KH = 8
NA_KW = 16
ROPE_THETA = 10000.0
EPS = 1e-6
EVEN_IN = (A_HEADS + 2 * A_KV + B_HEADS + 2 * B_KV) * HEAD_DIM
C_WIDTH = C_HEADS * HEAD_DIM
ODD_IN = 3 * C_WIDTH
NEG_INF = -1e30
LOG2E = float(np.log2(np.e))
QSCALE = HEAD_DIM ** -0.5 * LOG2E

LANES = 128
MXU_N = 256
GROUP = A_HEADS // A_KV
V_ROWS = HEAD_DIM + 16
KV_LANES = A_KV * HEAD_DIM
assert KV_LANES == LANES and B_KV == A_KV

F32 = jnp.float32
BF16 = jnp.bfloat16

VMEM_LIMIT = 56 * 1024 * 1024

TM = 512
TILE_ROWS = TM // GRID_W


def _cparams(*sem):
    return pltpu.CompilerParams(dimension_semantics=sem, vmem_limit_bytes=VMEM_LIMIT)


def _rms(x, g):
    ms = jnp.mean(x * x, axis=-1, keepdims=True)
    return x * lax.rsqrt(ms + EPS) * g


FFN_SPLITS = (0, 6 * MXU_N, D_FF)
assert D_FF % MXU_N == 0


def _ffn_apply(x, g, wg_ref, wu_ref, wd_ref):
    h = _rms(x, g).astype(BF16)
    acc = jnp.zeros_like(x)
    for lo, hi in zip(FFN_SPLITS[:-1], FFN_SPLITS[1:]):
        sl = slice(lo, hi)
        gate = jnp.dot(h, wg_ref[:, sl], preferred_element_type=F32)
        up = jnp.dot(h, wu_ref[:, sl], preferred_element_type=F32)
        act = (gate * jax.nn.sigmoid(gate) * up).astype(BF16)
        acc = acc + jnp.dot(act, wd_ref[sl, :], preferred_element_type=F32)
    return x + 0.5 * acc


N_CAST = 8
N_TILES = SEQ // TM
FFN_GRID = (N_CAST + N_TILES,)
FFN_SHAPES = [(D_MODEL, D_FF), (D_MODEL, D_FF), (D_FF, D_MODEL)]


def _tile(s):
    return jnp.maximum(s - N_CAST, 0)


def _weight_spec(shape, layer):
    rows, cols = shape
    assert rows % (16 * N_CAST) == 0
    return pl.BlockSpec((None, rows // N_CAST, cols),
                        lambda s: (layer, jnp.minimum(s, N_CAST - 1), 0))


def _weight_scratch(shapes):
    return [pltpu.VMEM(shape, BF16) for shape in shapes]


def _ffn_specs(layer):
    vec = pl.BlockSpec((None, 1, D_MODEL), lambda s: (layer, 0, 0))
    return [vec] + [_weight_spec(shape, layer) for shape in FFN_SHAPES]


def _cast_chunks(s, srcs, dsts):
    for src, dst in zip(srcs, dsts):
        n = src.shape[0]
        dst[pl.ds(pl.multiple_of(s * n, n), n), :] = src[...].astype(BF16)


def _split_heads_t(dst_ref, y, first_head, cols, ones_row=False):
    yt = y.T
    n = y.shape[0]
    for t in range(y.shape[1] // HEAD_DIM):
        dst_ref[first_head + t, 0:HEAD_DIM, cols] = yt[t * HEAD_DIM:(t + 1) * HEAD_DIM, :].astype(BF16)
        if ones_row:
            r = lax.broadcasted_iota(jnp.int32, (V_ROWS - HEAD_DIM, n), 0)
            dst_ref[first_head + t, HEAD_DIM:V_ROWS, cols] = jnp.where(r == 0, 1.0, 0.0).astype(BF16)


def _rope_factor_tables():
    lane = np.arange(LANES)
    d = lane % HEAD_DIM
    r = np.arange(ROWS, dtype=np.float64)[:, None]
    c = np.arange(GRID_W, dtype=np.float64)[:, None]
    inv_a = ROPE_THETA ** (-(2.0 * (d % 16)) / (HEAD_DIM // 2))
    sgn_a = np.where((d % 32) < 16, -1.0, 1.0)
    is_row = (d < 32)[None, :]
    a_tabs = [np.where(is_row, np.cos(r * inv_a), 0.0), np.where(is_row, sgn_a * np.sin(r * inv_a), 0.0),
              np.where(~is_row, np.cos(c * inv_a), 0.0), np.where(~is_row, sgn_a * np.sin(c * inv_a), 0.0)]
    inv_b = ROPE_THETA ** (-(2.0 * (d % 32)) / HEAD_DIM)
    sgn_b = np.where(d < 32, -1.0, 1.0)
    b_tabs = [np.cos(GRID_W * r * inv_b), sgn_b * np.sin(GRID_W * r * inv_b),
              np.cos(c * inv_b), sgn_b * np.sin(c * inv_b)]
    return [jnp.asarray(t, F32) for t in a_tabs + b_tabs]


def _rope_specs():
    row_tab = pl.BlockSpec((TILE_ROWS, LANES), lambda s: (_tile(s), 0))
    col_tab = pl.BlockSpec((GRID_W, LANES), lambda s: (0, 0))
    return [row_tab, row_tab, col_tab, col_tab] * 2


def _rope_tiles(grid_rows, arc, ars, acc_, acs, brc, brs, bcc, bcs):
    ca, sa, cb, sb = [], [], [], []
    for a in grid_rows:
        row = slice(a, a + 1)
        ca.append(arc[row, :] + acc_[...])
        sa.append(ars[row, :] + acs[...])
        rc, rs = brc[row, :], brs[row, :]
        cb.append(rc * bcc[...] - rs * bcs[...])
        sb.append(rs * bcc[...] + rc * bcs[...])
    cat = lambda parts: jnp.concatenate(parts, axis=0)
    return cat(ca), cat(sa), cat(cb), cat(sb)


EVEN_SPLIT = 2


def _even_in_kernel(x_ref, g1_ref, wg_ref, wu_ref, wd_ref, g_ref, w_ref, *refs):
    rest, (wg_sc, wu_sc, wd_sc, w_sc) = refs[:-4], refs[-4:]
    s = pl.program_id(0)

    @pl.when(s < N_CAST)
    def _():
        _cast_chunks(s, (wg_ref, wu_ref, wd_ref, w_ref), (wg_sc, wu_sc, wd_sc, w_sc))

    @pl.when(s >= N_CAST)
    def _():
        _even_in_tile(x_ref, g1_ref, wg_sc, wu_sc, wd_sc, g_ref, w_sc, *rest)


def _even_in_tile(x_ref, g1_ref, wg_ref, wu_ref, wd_ref, g_ref, w_ref, gq_ref, gk_ref, gm_ref,
                  arc, ars, acc_, acs, brc, brs, bcc, bcs,
                  x1_ref, qa_ref, ka_ref, va_ref, qb_ref, kb_ref, vb_ref):
    sub = TM // EVEN_SPLIT
    lane = lax.broadcasted_iota(jnp.int32, (sub, LANES), 1)
    gm = gm_ref[...]

    def head_norm(y, gain):
        sq = y * y
        hi = sq.astype(BF16)
        lo = (sq - hi.astype(F32)).astype(BF16)
        ms = jnp.dot(jnp.concatenate([hi, lo], axis=1), gm, preferred_element_type=F32)
        return y * lax.rsqrt(ms + EPS) * gain

    def rope(y, c, s, half):
        fwd = pltpu.roll(y, LANES - half, 1)
        bwd = pltpu.roll(y, half, 1)
        partner = jnp.where((lane % (2 * half)) < half, fwd, bwd)
        return y * c + partner * s

    for part in range(EVEN_SPLIT):
        rows = slice(part * sub, (part + 1) * sub)
        x1 = _ffn_apply(x_ref[rows, :], g1_ref[...], wg_ref, wu_ref, wd_ref)
        x1_ref[rows, :] = x1
        h = _rms(x1, g_ref[...]).astype(BF16)
        proj = jnp.dot(h, w_ref[...], preferred_element_type=F32)
        grid_rows = range(part * sub // GRID_W, (part + 1) * sub // GRID_W)
        ca, sa, cb, sb = _rope_tiles(grid_rows, arc, ars, acc_, acs, brc, brs, bcc, bcs)
        col = 0
        for j in range(A_HEADS // 2):
            y = rope(head_norm(proj[:, col:col + LANES], gq_ref[...]), ca, sa, HEAD_DIM // 4)
            _split_heads_t(qa_ref, y * QSCALE, 2 * j, rows)
            col += LANES
        y = rope(head_norm(proj[:, col:col + LANES], gk_ref[...]), ca, sa, HEAD_DIM // 4)
        ka_ref[rows, :] = y.astype(BF16)
        col += LANES
        _split_heads_t(va_ref, proj[:, col:col + LANES], 0, rows, ones_row=True)
        col += LANES
        for j in range(B_HEADS // 2):
            y = rope(proj[:, col:col + LANES], cb, sb, HEAD_DIM // 2) * QSCALE
            _split_heads_t(qb_ref, y, 2 * j, rows)
            col += LANES
        y = rope(proj[:, col:col + LANES], cb, sb, HEAD_DIM // 2)
        kb_ref[rows, :] = y.astype(BF16)
        col += LANES
        _split_heads_t(vb_ref, proj[:, col:col + LANES], 0, rows, ones_row=True)


def _even_in(x, layer, i, g1, wg, wu, wd, g, w, gq, gk, gm, rope_tabs):
    const = lambda s: (0, 0)
    row = lambda s: (_tile(s), 0)
    k_spec = pl.BlockSpec((TM, KV_LANES), row)
    k_shape = jax.ShapeDtypeStruct((SEQ, KV_LANES), BF16)
    t_spec = lambda n, rows: pl.BlockSpec((n, rows, TM), lambda s: (0, 0, _tile(s)))
    t_shape = lambda n, rows: jax.ShapeDtypeStruct((n, rows, SEQ), BF16)
    x_spec = pl.BlockSpec((TM, D_MODEL), row)
    return pl.pallas_call(
        _even_in_kernel,
        grid=FFN_GRID,
        scratch_shapes=_weight_scratch(FFN_SHAPES + [(D_MODEL, EVEN_IN)]),
        in_specs=[x_spec] + _ffn_specs(layer) + [
            pl.BlockSpec((None, 1, D_MODEL), lambda s: (layer, 0, 0)),
            _weight_spec((D_MODEL, EVEN_IN), i),
            pl.BlockSpec((1, LANES), const),
            pl.BlockSpec((1, LANES), const),
            pl.BlockSpec((2 * LANES, LANES), const),
        ] + _rope_specs(),
        out_specs=[x_spec, t_spec(A_HEADS, HEAD_DIM), k_spec, t_spec(A_KV, V_ROWS),
                   t_spec(B_HEADS, HEAD_DIM), k_spec, t_spec(B_KV, V_ROWS)],
        out_shape=[jax.ShapeDtypeStruct((SEQ, D_MODEL), F32),
                   t_shape(A_HEADS, HEAD_DIM), k_shape, t_shape(A_KV, V_ROWS),
                   t_shape(B_HEADS, HEAD_DIM), k_shape, t_shape(B_KV, V_ROWS)],
        compiler_params=_cparams("arbitrary"),
        name="even_in",
    )(x, g1, wg, wu, wd, g, w, gq, gk, gm, *rope_tabs)


A_TQ = 2048
A_TK = 256
A_UNROLL = 4
A_REF_KEYS = 16
A_MAX_SUM = 2.0 ** 100


def _attn_a_kernel(qt_ref, k_ref, vt_ref, o_ref, qt_sc, acc_sc):
    j = pl.program_id(0)
    cols = GROUP * A_TQ
    for jj in range(A_KV):
        rows = slice(jj * HEAD_DIM, (jj + 1) * HEAD_DIM)
        for g in range(GROUP):
            qt_sc[rows, g * A_TQ:(g + 1) * A_TQ] = jnp.where(j == jj, qt_ref[g], 0).astype(BF16)

    def scores(off, n=A_TK):
        return jnp.dot(k_ref[pl.ds(off, n), :], qt_sc[...], preferred_element_type=F32)

    def weighted_values(off, pt):
        return jnp.dot(vt_ref[:, pl.ds(off, A_TK)], pt, preferred_element_type=F32)

    def finish():
        acc = acc_sc[...]
        ot = acc[:HEAD_DIM] / acc[HEAD_DIM:HEAD_DIM + 1]
        head = lambda g: ot[:, g * A_TQ:(g + 1) * A_TQ]
        for p in range(GROUP // 2):
            pair = jnp.concatenate([head(2 * p), head(2 * p + 1)], axis=0)
            o_ref[:, p * LANES:(p + 1) * LANES] = pair.T.astype(BF16)

    m_ref = jnp.max(scores(0, A_REF_KEYS), axis=0, keepdims=True)
    acc_sc[...] = jnp.zeros((V_ROWS, cols), F32)

    def fast_body(c, carry):
        for u in range(A_UNROLL):
            off = pl.multiple_of((c * A_UNROLL + u) * A_TK, A_TK)
            pt = jnp.exp2(scores(off) - m_ref).astype(BF16)
            acc_sc[...] += weighted_values(off, pt)
        return carry

    lax.fori_loop(0, SEQ // (A_TK * A_UNROLL), fast_body, 0)
    lead_ok = jnp.max(jnp.abs(acc_sc[...])) < A_MAX_SUM

    @pl.when(lead_ok)
    def _():
        finish()

    @pl.when(jnp.logical_not(lead_ok))
    def _():
        acc_sc[...] = jnp.zeros((V_ROWS, cols), F32)

        def safe_body(c, m_prev):
            off = pl.multiple_of(c * A_TK, A_TK)
            st = scores(off)
            m_new = jnp.maximum(m_prev, jnp.max(st, axis=0, keepdims=True))
            alpha = jnp.exp2(m_prev - m_new)
            pt = jnp.exp2(st - m_new).astype(BF16)
            acc_sc[...] = acc_sc[...] * alpha + weighted_values(off, pt)
            return m_new

        lax.fori_loop(0, SEQ // A_TK, safe_body, jnp.full((1, cols), -jnp.inf, F32))
        finish()


def _attn_a(qt, k, vt):
    cols = GROUP * A_TQ
    return pl.pallas_call(
        _attn_a_kernel,
        grid=(A_KV, SEQ // A_TQ),
        in_specs=[
            pl.BlockSpec((GROUP, HEAD_DIM, A_TQ), lambda j, i: (j, 0, i)),
            pl.BlockSpec((SEQ, KV_LANES), lambda j, i: (0, 0)),
            pl.BlockSpec((None, V_ROWS, SEQ), lambda j, i: (j, 0, 0)),
        ],
        out_specs=pl.BlockSpec((A_TQ, GROUP * HEAD_DIM), lambda j, i: (i, j)),
        out_shape=jax.ShapeDtypeStruct((SEQ, A_HEADS * HEAD_DIM), BF16),
        scratch_shapes=[
            pltpu.VMEM((KV_LANES, cols), BF16),
            pltpu.VMEM((V_ROWS, cols), F32),
        ],
        compiler_params=_cparams("arbitrary", "arbitrary"),
        name="attn_a",
    )(qt, k, vt)


N_QBLK = SEQ // Q_BLOCK
B_TQ = 2 * Q_BLOCK
B_KEYS = B_TQ + 2 * Q_BLOCK


def _attn_b_kernel(sink_ref, qt_ref, kl_ref, kc_ref, kr_ref, vl_ref, vc_ref, vr_ref, o_ref):
    i = pl.program_id(0)
    k = jnp.concatenate([kl_ref[...], kc_ref[...], kr_ref[...]], axis=0)
    vts = [jnp.concatenate([vl_ref[j], vc_ref[j], vr_ref[j]], axis=1)
           for j in range(B_KV)]
    kj = lax.broadcasted_iota(jnp.int32, (B_KEYS, B_TQ), 0)
    qi = lax.broadcasted_iota(jnp.int32, (B_KEYS, B_TQ), 1)
    kpos = i * B_TQ - Q_BLOCK + kj
    valid = (jnp.abs(qi + Q_BLOCK - kj) <= WINDOW) & (kpos >= 0) & (kpos < SEQ)
    zeros = jnp.zeros((HEAD_DIM, GROUP * B_TQ), BF16)
    head_cols = lambda g: slice(g * B_TQ, (g + 1) * B_TQ)

    def scores(j):
        qt = jnp.concatenate([qt_ref[j * GROUP + g] for g in range(GROUP)], axis=1)
        parts = [qt if jj == j else zeros for jj in range(B_KV)]
        st = jnp.dot(k, jnp.concatenate(parts, axis=0), preferred_element_type=F32)
        return jnp.concatenate(
            [jnp.where(valid, st[:, head_cols(g)], NEG_INF) for g in range(GROUP)], axis=1)

    def attend(shift_by_max):
        res = []
        st_next = scores(0)
        for j in range(B_KV):
            sk = jnp.concatenate(
                [jnp.full((1, B_TQ), sink_ref[j * GROUP + g] * LOG2E, F32) for g in range(GROUP)],
                axis=1)
            st = st_next
            if j + 1 < B_KV:
                st_next = scores(j + 1)
            m = jnp.maximum(jnp.max(st, axis=0, keepdims=True), sk) if shift_by_max else sk
            pt = jnp.exp2(st - m).astype(BF16)
            pv = jnp.dot(vts[j], pt, preferred_element_type=F32)
            res.append((pv, jnp.exp2(sk - m)))
        return res

    def write(res):
        for j, (pv, sink) in enumerate(res):
            ot = pv[:HEAD_DIM] / (pv[HEAD_DIM:HEAD_DIM + 1] + sink)
            for p in range(GROUP // 2):
                pair = jnp.concatenate([ot[:, head_cols(2 * p)], ot[:, head_cols(2 * p + 1)]],
                                       axis=0)
                lanes = slice((j * GROUP // 2 + p) * LANES, (j * GROUP // 2 + p + 1) * LANES)
                o_ref[:, lanes] = pair.T.astype(BF16)

    fast = attend(shift_by_max=False)
    bound = functools.reduce(jnp.maximum, [jnp.max(jnp.abs(pv)) for pv, _ in fast])
    ok = bound < A_MAX_SUM

    @pl.when(ok)
    def _():
        write(fast)

    @pl.when(jnp.logical_not(ok))
    def _():
        write(attend(shift_by_max=True))


def _attn_b(sink, qt, k, vt):
    per_step = B_TQ // Q_BLOCK
    side = lambda d: (lambda i, s: jnp.clip(per_step * i + d, 0, N_QBLK - 1))
    left, right = side(-1), side(per_step)
    k_side = lambda f: pl.BlockSpec((Q_BLOCK, KV_LANES), lambda i, s: (f(i, s), 0))
    v_side = lambda f: pl.BlockSpec((B_KV, V_ROWS, Q_BLOCK), lambda i, s: (0, 0, f(i, s)))
    return pl.pallas_call(
        _attn_b_kernel,
        grid_spec=pltpu.PrefetchScalarGridSpec(
            num_scalar_prefetch=1,
            grid=(SEQ // B_TQ,),
            in_specs=[
                pl.BlockSpec((B_HEADS, HEAD_DIM, B_TQ), lambda i, s: (0, 0, i)),
                k_side(left),
                pl.BlockSpec((B_TQ, KV_LANES), lambda i, s: (i, 0)),
                k_side(right),
                v_side(left),
                pl.BlockSpec((B_KV, V_ROWS, B_TQ), lambda i, s: (0, 0, i)),
                v_side(right),
            ],
            out_specs=pl.BlockSpec((B_TQ, B_HEADS * HEAD_DIM), lambda i, s: (i, 0)),
        ),
        out_shape=jax.ShapeDtypeStruct((SEQ, B_HEADS * HEAD_DIM), BF16),
        compiler_params=_cparams("arbitrary"),
        name="attn_b",
    )(sink, qt, k, k, k, vt, vt, vt)


def _mix_out_kernel(x_ref, *refs, n_in, final_norm):
    o_refs = refs[:n_in]
    w_ref, g2_ref, wg_ref, wu_ref, wd_ref, fg_ref, y_ref = refs[n_in:-4]
    w_sc, wg_sc, wu_sc, wd_sc = refs[-4:]
    s = pl.program_id(0)

    @pl.when(s < N_CAST)
    def _():
        _cast_chunks(s, (w_ref, wg_ref, wu_ref, wd_ref), (w_sc, wg_sc, wu_sc, wd_sc))

    @pl.when(s >= N_CAST)
    def _():
        o = jnp.concatenate([r[...] for r in o_refs], axis=-1)
        x2 = x_ref[...] + jnp.dot(o, w_sc[...], preferred_element_type=F32)
        y = _ffn_apply(x2, g2_ref[...], wg_sc, wu_sc, wd_sc)
        if final_norm:
            y = _rms(y, fg_ref[...])
        y_ref[...] = y


def _mix_out(x, outs, layer, i, w, g2, wg, wu, wd, fg, final_norm):
    row = lambda s: (_tile(s), 0)
    x_spec = pl.BlockSpec((TM, D_MODEL), row)
    in_specs = [x_spec]
    for o in outs:
        in_specs.append(pl.BlockSpec((TM, o.shape[1]), row))
    in_specs.append(_weight_spec((D_MODEL, D_MODEL), i))
    in_specs += _ffn_specs(layer)
    in_specs.append(pl.BlockSpec((1, D_MODEL), lambda s: (0, 0)))
    return pl.pallas_call(
        functools.partial(_mix_out_kernel, n_in=len(outs), final_norm=final_norm),
        grid=FFN_GRID,
        scratch_shapes=_weight_scratch([(D_MODEL, D_MODEL)] + FFN_SHAPES),
        in_specs=in_specs,
        out_specs=x_spec,
        out_shape=jax.ShapeDtypeStruct((SEQ, D_MODEL), F32),
        compiler_params=_cparams("arbitrary"),
        name="mix_out",
    )(x, *outs, w, g2, wg, wu, wd, fg)


def _odd_in_kernel(x_ref, g1_ref, wg_ref, wu_ref, wd_ref, g_ref, w_ref, x1_ref, q_ref, k_ref, v_ref,
                   wg_sc, wu_sc, wd_sc, w_sc):
    s = pl.program_id(0)

    @pl.when(s < N_CAST)
    def _():
        _cast_chunks(s, (wg_ref, wu_ref, wd_ref, w_ref), (wg_sc, wu_sc, wd_sc, w_sc))

    @pl.when(s >= N_CAST)
    def _():
        x1 = _ffn_apply(x_ref[...], g1_ref[...], wg_sc, wu_sc, wd_sc)
        x1_ref[...] = x1
        h = _rms(x1, g_ref[...]).astype(BF16)
        for idx, (dst, scale) in enumerate(((q_ref, QSCALE), (k_ref, None), (v_ref, None))):
            y = jnp.dot(h, w_sc[:, idx * C_WIDTH:(idx + 1) * C_WIDTH],
                        preferred_element_type=F32)
            if scale is not None:
                y = y * scale
            dst[...] = y.astype(BF16)


def _odd_in(x, layer, i, g1, wg, wu, wd, g, w):
    x_spec = pl.BlockSpec((TM, D_MODEL), lambda s: (_tile(s), 0))
    hm = pl.BlockSpec((TM, C_WIDTH), lambda s: (_tile(s), 0))
    hs = jax.ShapeDtypeStruct((SEQ, C_WIDTH), BF16)
    return pl.pallas_call(
        _odd_in_kernel,
        grid=FFN_GRID,
        scratch_shapes=_weight_scratch(FFN_SHAPES + [(D_MODEL, ODD_IN)]),
        in_specs=[x_spec] + _ffn_specs(layer) + [
            pl.BlockSpec((None, 1, D_MODEL), lambda s: (layer, 0, 0)),
            _weight_spec((D_MODEL, ODD_IN), i),
        ],
        out_specs=[x_spec, hm, hm, hm],
        out_shape=[jax.ShapeDtypeStruct((SEQ, D_MODEL), F32), hs, hs, hs],
        compiler_params=_cparams("arbitrary"),
        name="odd_in",
    )(x, g1, wg, wu, wd, g, w)


KH = min(NA_KH, ROWS)
C_R = 8
C_WIN = C_R + KH
C_KEYS = KH * GRID_W
assert KH % 2 == 0 and LANES == 2 * GRID_W


def _row_start(r):
    return jnp.clip(r - KH // 2, 0, ROWS - KH)


def _win_start(i):
    return jnp.clip(i * C_R - KH // 2, 0, ROWS - C_WIN)


def _attn_c_kernel(q_ref, k_ref, v_ref, bias_ref, o_ref):
    i = pl.program_id(0)
    base = _win_start(i)
    low = lax.broadcasted_iota(jnp.int32, (GRID_W, LANES), 1) < HEAD_DIM

    def row_scores(t):
        r = i * C_R + t
        rs = _row_start(r)
        koff = pl.multiple_of((rs - base) * GRID_W, GRID_W)
        first = rs - r + (NA_KH - 1)
        parts = []
        for pr in range(C_HEADS // 2):
            lanes = slice(pr * LANES, (pr + 1) * LANES)
            q = q_ref[t * GRID_W:(t + 1) * GRID_W, lanes]
            q2 = jnp.concatenate([jnp.where(low, q, 0), jnp.where(low, 0, q)], axis=0)
            parts.append(lax.dot_general(q2.astype(BF16), k_ref[pl.ds(koff, C_KEYS), lanes],
                                         (((1,), (1,)), ((), ())), preferred_element_type=F32))
        bias = jnp.concatenate(
            [jnp.concatenate([bias_ref[h, first + 2 * a] for a in range(KH // 2)], axis=1)
             for h in range(C_HEADS)], axis=0)
        return jnp.concatenate(parts, axis=0) + bias, koff

    nxt = row_scores(0)
    for t in range(C_R):
        s, koff = nxt
        if t + 1 < C_R:
            nxt = row_scores(t + 1)
        m = jnp.max(s, axis=1, keepdims=True)
        p = jnp.exp2(s - m)
        denom = jnp.sum(p, axis=1, keepdims=True)
        pb = p.astype(BF16)
        for pr in range(C_HEADS // 2):
            lanes = slice(pr * LANES, (pr + 1) * LANES)
            rows = slice(pr * LANES, (pr + 1) * LANES)
            o2 = jnp.dot(pb[rows], v_ref[pl.ds(koff, C_KEYS), lanes],
                         preferred_element_type=F32) / denom[rows]
            o_ref[t * GRID_W:(t + 1) * GRID_W, lanes] = jnp.where(
                low, o2[:GRID_W], o2[GRID_W:]).astype(BF16)


def _attn_c(q, k, v, bias):
    q_spec = pl.BlockSpec((C_R * GRID_W, C_WIDTH), lambda i: (i, 0))
    win_spec = pl.BlockSpec((pl.Element(C_WIN * GRID_W), pl.Element(C_WIDTH)),
                            lambda i: (_win_start(i) * GRID_W, 0))
    bias_spec = pl.BlockSpec(bias.shape, lambda i: (0, 0, 0, 0), pipeline_mode=pl.Buffered(1))
    return pl.pallas_call(
        _attn_c_kernel,
        grid=(ROWS // C_R,),
        in_specs=[q_spec, win_spec, win_spec, bias_spec],
        out_specs=q_spec,
        out_shape=jax.ShapeDtypeStruct((SEQ, C_WIDTH), BF16),
        compiler_params=_cparams("arbitrary"),
        name="attn_c",
    )(q, k, v, bias)


def _neighbourhood_bias(rel_bias):
    cols = np.arange(GRID_W)
    col_start = np.clip(cols - NA_KW // 2, 0, GRID_W - NA_KW)
    kc = np.arange(GRID_W)
    in_win = (kc[None, :] >= col_start[:, None]) & (kc[None, :] < col_start[:, None] + NA_KW)
    n_rel = 2 * NA_KW - 1
    col_idx = np.where(in_win, kc[None, :] - cols[:, None] + (NA_KW - 1), n_rel)
    onehot = (col_idx[None] == np.arange(n_rel + 1)[:, None, None]).astype(np.float32)
    pair_onehot = np.zeros((2, n_rel + 1, GRID_W, 2, GRID_W), np.float32)
    pair_onehot[0, :, :, 0, :] = onehot
    pair_onehot[1, :, :, 1, :] = onehot
    pair_onehot = pair_onehot.reshape(2 * (n_rel + 1), GRID_W, 2 * GRID_W)
    table = jnp.concatenate(
        [rel_bias * LOG2E, jnp.full(rel_bias.shape[:2] + (1,), NEG_INF, F32)], axis=-1)
    table = jnp.concatenate([table[:, :-1], table[:, 1:]], axis=-1)
    return jnp.einsum('hem,mcn->hecn', table, jnp.asarray(pair_onehot),
                      precision=lax.Precision.HIGHEST)


def kernel(x, ffn1_norm, ffn1_w_gate, ffn1_w_up, ffn1_w_down, mix_norm, ffn2_norm, ffn2_w_gate,
           ffn2_w_up, ffn2_w_down, even_w_in, a_q_norm, a_k_norm, b_sink, even_w_out, odd_w_qkv,
           c_rel_bias, odd_w_out, final_norm):
    assert x.shape == (1, SEQ, D_MODEL)
    xs = x.reshape(SEQ, D_MODEL)
    rope_tabs = _rope_factor_tables()
    reps = LANES // HEAD_DIM
    block_mean = np.kron(np.eye(reps), np.full((HEAD_DIM, HEAD_DIM), 1.0 / HEAD_DIM))
    group_mean = jnp.asarray(np.concatenate([block_mean, block_mean], axis=0), BF16)
    fg = final_norm.reshape(1, D_MODEL)
    vec = lambda v: v.reshape(DEPTH, 1, D_MODEL)
    g1, gmix, g2 = vec(ffn1_norm), vec(mix_norm), vec(ffn2_norm)
    w1 = (ffn1_w_gate, ffn1_w_up, ffn1_w_down)
    w2 = (ffn2_w_gate, ffn2_w_up, ffn2_w_down)

    for layer in range(DEPTH):
        i = layer // 2
        last = layer == DEPTH - 1
        if layer % 2 == 0:
            xs, qa, ka, va, qb, kb, vb = _even_in(
                xs, layer, i, g1, *w1, gmix, even_w_in,
                jnp.tile(a_q_norm[i], reps).reshape(1, LANES),
                jnp.tile(a_k_norm[i], reps).reshape(1, LANES), group_mean, rope_tabs)
            outs = [_attn_a(qa, ka, va), _attn_b(b_sink[i], qb, kb, vb)]
            w_out = even_w_out
        else:
            xs, q, k, v = _odd_in(xs, layer, i, g1, *w1, gmix, odd_w_qkv)
            outs = [_attn_c(q, k, v, _neighbourhood_bias(c_rel_bias[i]))]
            w_out = odd_w_out
        xs = _mix_out(xs, outs, layer, i, w_out, g2, *w2, fg, last)
    return xs.reshape(1, SEQ, D_MODEL)
```
